```python
import math
import jax, jax.numpy as jnp
from jax import lax
import numpy as np

D_MODEL = 1024
BATCH = 8
SEQ = 4096
DEPTH = 2

D_MIX = D_MODEL
ML_HEADS = 4
ML_DH = 96
ML_W = ML_HEADS * ML_DH
ML_CHUNK = 64
ML_CONV = 4
SB_HEADS = 6
SB_DH = 64
SB_W = SB_HEADS * SB_DH
SB_BLOCK = 128
S5_GROUP_CH = 16
S5_W = D_MIX - ML_W - SB_W
S5_GROUPS = S5_W // S5_GROUP_CH
S5_STATE = 64
S5_DT_MIN = 1e-3
S5_DT_MAX = 1e-1
D_FF = 2816
FFN_CONV = 3
EPS = 1e-6
IN_WIDTHS = (2 * ML_W, ML_W, ML_W, 2 * ML_HEADS, SB_W, SB_W, SB_W, S5_W)
N_IN = 4 * ML_W + 2 * ML_HEADS + 3 * SB_W + S5_W

kernel_name = 'hymba_style_mlstm_stickbreak_s5_hybrid'


def rmsnorm(x, g):
    xf = x.astype(jnp.float32)
    y = xf * lax.rsqrt(jnp.mean(xf * xf, axis=-1, keepdims=True) + EPS)
    return (y * g.astype(jnp.float32)).astype(x.dtype)


def causal_dwconv(x, w):
    K, C = w.shape
    return lax.conv_general_dilated(
        x, w[:, None, :].astype(x.dtype), window_strides=(1,), padding=[(K - 1, 0)],
        dimension_numbers=('NWC', 'WIO', 'NWC'), feature_group_count=C)


def split_cols(z, widths):
    out, off = [], 0
    for w in widths:
        out.append(z[..., off:off + w])
        off += w
    return out


def mlstm(q, k, v, i_pre, f_pre):
    B, L, H, Dh = q.shape
    C = ML_CHUNK
    NC = L // C
    k = k * (Dh ** -0.5)
    qc = q.reshape(B, NC, C, H, Dh)
    kc = k.reshape(B, NC, C, H, Dh)
    vc = v.reshape(B, NC, C, H, Dh)
    ic = i_pre.reshape(B, NC, C, H)
    b = jnp.cumsum(jax.nn.log_sigmoid(f_pre).reshape(B, NC, C, H), axis=2)
    b_tot = b[:, :, -1]
    w_log = b_tot[:, :, None] - b + ic
    m_loc = jnp.max(w_log, axis=2)
    wexp = jnp.exp(w_log - m_loc[:, :, None])
    dC = jnp.einsum('bnchk,bnchv->bnhkv', wexp[..., None] * kc, vc)
    dn = jnp.einsum('bnch,bnchk->bnhk', wexp, kc)

    def step(carry, inp):
        Cs, ns, m = carry
        dC_j, dn_j, m_j, bt_j = inp
        m_new = jnp.maximum(bt_j + m, m_j)
        a = jnp.exp(bt_j + m - m_new)
        c = jnp.exp(m_j - m_new)
        Cn = a[..., None, None] * Cs + c[..., None, None] * dC_j
        nn = a[..., None] * ns + c[..., None] * dn_j
        return (Cn, nn, m_new), (Cs, ns, m)

    init = (jnp.zeros((B, H, Dh, Dh), jnp.float32), jnp.zeros((B, H, Dh), jnp.float32),
            jnp.zeros((B, H), jnp.float32))
    xs = (jnp.moveaxis(dC, 1, 0), jnp.moveaxis(dn, 1, 0), jnp.moveaxis(m_loc, 1, 0), jnp.moveaxis(b_tot, 1, 0))
    _, (C_prev, n_prev, m_prev) = lax.scan(step, init, xs)
    C_prev = jnp.moveaxis(C_prev, 0, 1)
    n_prev = jnp.moveaxis(n_prev, 0, 1)
    m_prev = jnp.moveaxis(m_prev, 0, 1)

    inter_log = b + m_prev[:, :, None]
    Dlog = b[:, :, :, None, :] - b[:, :, None, :, :] + ic[:, :, None, :, :]
    causal = jnp.tril(jnp.ones((C, C), dtype=bool))[None, None, :, :, None]
    Dlog = jnp.where(causal, Dlog, -jnp.inf)
    m_t = jnp.maximum(inter_log, jnp.max(Dlog, axis=3))
    s = jnp.einsum('bnthd,bnshd->bntsh', qc, kc) * jnp.exp(Dlog - m_t[:, :, :, None, :])
    inter_scale = jnp.exp(inter_log - m_t)
    num = (jnp.einsum('bntsh,bnshd->bnthd', s, vc)
           + inter_scale[..., None] * jnp.einsum('bnthk,bnhkv->bnthv', qc, C_prev))
    den = jnp.sum(s, axis=3) + inter_scale * jnp.einsum('bnthk,bnhk->bnth', qc, n_prev)
    h = num / jnp.maximum(jnp.abs(den), jnp.exp(-m_t))[..., None]
    return h.reshape(B, L, H, Dh)


def stick_breaking(q, k, v):
    B, L, H, Dh = q.shape
    scale = Dh ** -0.5
    outs = []
    for blk in range(L // SB_BLOCK):
        q0 = blk * SB_BLOCK
        kv_len = q0 + SB_BLOCK
        z = jnp.einsum('bthd,bshd->bhts', q[:, q0:kv_len], k[:, :kv_len]) * scale
        t_pos = q0 + jnp.arange(SB_BLOCK)
        s_pos = jnp.arange(kv_len)
        causal = s_pos[None, :] < t_pos[:, None]
        log_1mb = jnp.where(causal, jax.nn.log_sigmoid(-z), 0.0)
        suffix = lax.cumsum(log_1mb, axis=3, reverse=True) - log_1mb
        a = jnp.where(causal, jnp.exp(jax.nn.log_sigmoid(z) + suffix), 0.0)
        outs.append(jnp.einsum('bhts,bshd->bthd', a, v[:, :kv_len]))
    return jnp.concatenate(outs, axis=1)


def s5(u, a_re, a_im, log_dt, b_re, b_im, c_re, c_im, d, glu_w, glu_b):
    f32 = jnp.float32
    Bsz, L, _ = u.shape
    ug = u.reshape(Bsz, L, S5_GROUPS, S5_GROUP_CH)
    lam = lax.complex(jnp.minimum(a_re.astype(f32), -1e-4), a_im.astype(f32))
    dt = jnp.exp(log_dt.astype(f32))[:, None]
    lam_bar = jnp.exp(lam * dt)
    b_bar = ((lam_bar - 1.0) / lam)[..., None] * lax.complex(b_re.astype(f32), b_im.astype(f32))
    bu = jnp.einsum('gph,blgh->blgp', b_bar, ug.astype(jnp.complex64))
    a_full = jnp.broadcast_to(lam_bar, bu.shape)

    def combine(e1, e2):
        a1, b1 = e1
        a2, b2 = e2
        return a1 * a2, a2 * b1 + b2

    _, states = lax.associative_scan(combine, (a_full, bu), axis=1)
    y = jnp.einsum('ghp,blgp->blgh', lax.complex(c_re.astype(f32), c_im.astype(f32)), states).real
    y = y + d.astype(f32) * ug
    y = jax.nn.gelu(y.reshape(Bsz, L, S5_W))
    return y * jax.nn.sigmoid(y @ glu_w.astype(f32) + glu_b.astype(f32))


def setup_inputs(seed: int = 0) -> dict:
    key = jax.random.key(seed)
    ks = jax.random.split(key, 32)
    f32 = jnp.float32
    nrm = lambda k, shape: jax.random.normal(k, shape, f32)
    gain = lambda k, shape: 1.0 + 0.02 * nrm(k, shape)
    L_ = DEPTH
    f_bias = jnp.linspace(3.0, 6.0, ML_HEADS, dtype=f32)
    ml_gate_b = jnp.concatenate([0.1 * nrm(ks[5], (L_, ML_HEADS)),
                                 f_bias[None, :] + 0.1 * nrm(ks[6], (L_, ML_HEADS))], axis=-1)
    a_im0 = math.pi * jnp.arange(S5_STATE, dtype=f32)
    return {
        'x': nrm(ks[0], (BATCH, SEQ, D_MODEL)),
        'norm_mix_g': gain(ks[1], (L_, D_MODEL)),
        'w_in': nrm(ks[2], (L_, D_MODEL, N_IN)) * D_MODEL ** -0.5,
        'ml_conv_w': nrm(ks[3], (L_, ML_CONV, 2 * ML_W)) * ML_CONV ** -0.5,
        'ml_conv_b': 0.02 * nrm(ks[4], (L_, 2 * ML_W)),
        'ml_gate_b': ml_gate_b,
        'ml_out_g': gain(ks[7], (L_, ML_HEADS, ML_DH)),
        'sb_q_g': gain(ks[8], (L_, SB_DH)),
        'sb_k_g': gain(ks[9], (L_, SB_DH)),
        'sb_out_g': gain(ks[10], (L_, SB_HEADS, SB_DH)),
        's5_a_re': -0.5 + 0.01 * nrm(ks[11], (L_, S5_GROUPS, S5_STATE)),
        's5_a_im': a_im0 + 0.01 * nrm(ks[12], (L_, S5_GROUPS, S5_STATE)),
        's5_log_dt': jax.random.uniform(ks[13], (L_, S5_GROUPS), f32, math.log(S5_DT_MIN), math.log(S5_DT_MAX)),
        's5_b_re': nrm(ks[14], (L_, S5_GROUPS, S5_STATE, S5_GROUP_CH)) * (2 * S5_GROUP_CH) ** -0.5,
        's5_b_im': nrm(ks[15], (L_, S5_GROUPS, S5_STATE, S5_GROUP_CH)) * (2 * S5_GROUP_CH) ** -0.5,
        's5_c_re': nrm(ks[16], (L_, S5_GROUPS, S5_GROUP_CH, S5_STATE)) * (2 * S5_STATE) ** -0.5,
        's5_c_im': nrm(ks[17], (L_, S5_GROUPS, S5_GROUP_CH, S5_STATE)) * (2 * S5_STATE) ** -0.5,
        's5_d': nrm(ks[18], (L_, S5_GROUPS, S5_GROUP_CH)),
        's5_glu_w': nrm(ks[19], (L_, S5_W, S5_W)) * S5_W ** -0.5,
        's5_glu_b': 0.02 * nrm(ks[20], (L_, S5_W)),
        's5_out_g': gain(ks[21], (L_, S5_W)),
        'w_out': nrm(ks[22], (L_, D_MIX, D_MODEL)) * D_MIX ** -0.5,
        'norm_ffn_g': gain(ks[23], (L_, D_MODEL)),
        'ffn_w_up': nrm(ks[24], (L_, D_MODEL, 2 * D_FF)) * D_MODEL ** -0.5,
        'ffn_conv_w': nrm(ks[25], (L_, FFN_CONV, D_FF)) * FFN_CONV ** -0.5,
        'ffn_w_down': nrm(ks[26], (L_, D_FF, D_MODEL)) * D_FF ** -0.5,
    }


def reference(x, norm_mix_g, w_in, ml_conv_w, ml_conv_b, ml_gate_b, ml_out_g, sb_q_g, sb_k_g, sb_out_g,
              s5_a_re, s5_a_im, s5_log_dt, s5_b_re, s5_b_im, s5_c_re, s5_c_im, s5_d, s5_glu_w, s5_glu_b,
              s5_out_g, w_out, norm_ffn_g, ffn_w_up, ffn_conv_w, ffn_w_down):
    f32 = jnp.float32
    Bsz, L, _ = x.shape
    dtype = x.dtype
    for l in range(DEPTH):
        h = rmsnorm(x, norm_mix_g[l])
        z = h @ w_in[l]
        ml_qk, ml_v, ml_o, ml_if, sb_q, sb_k, sb_v, s5_u = split_cols(z, IN_WIDTHS)

        qk = jax.nn.silu(causal_dwconv(ml_qk, ml_conv_w[l]) + ml_conv_b[l])
        mq, mk = qk[..., :ML_W], qk[..., ML_W:]
        gates = (ml_if + ml_gate_b[l]).astype(f32)
        hm = mlstm(mq.reshape(Bsz, L, ML_HEADS, ML_DH).astype(f32),
                   mk.reshape(Bsz, L, ML_HEADS, ML_DH).astype(f32),
                   ml_v.reshape(Bsz, L, ML_HEADS, ML_DH).astype(f32),
                   gates[..., :ML_HEADS], gates[..., ML_HEADS:])
        hm = rmsnorm(hm, ml_out_g[l]).reshape(Bsz, L, ML_W) * jax.nn.sigmoid(ml_o.astype(f32))

        sq = rmsnorm(sb_q.reshape(Bsz, L, SB_HEADS, SB_DH).astype(f32), sb_q_g[l])
        sk = rmsnorm(sb_k.reshape(Bsz, L, SB_HEADS, SB_DH).astype(f32), sb_k_g[l])
        hs = stick_breaking(sq, sk, sb_v.reshape(Bsz, L, SB_HEADS, SB_DH).astype(f32))
        hs = rmsnorm(hs, sb_out_g[l]).reshape(Bsz, L, SB_W)

        h5 = s5(s5_u.astype(f32), s5_a_re[l], s5_a_im[l], s5_log_dt[l], s5_b_re[l], s5_b_im[l],
                s5_c_re[l], s5_c_im[l], s5_d[l], s5_glu_w[l], s5_glu_b[l])
        h5 = rmsnorm(h5, s5_out_g[l])

        mix = jnp.concatenate([hm, hs, h5], axis=-1).astype(dtype)
        x = x + mix @ w_out[l]

        h = rmsnorm(x, norm_ffn_g[l])
        up = h @ ffn_w_up[l]
        g, v = up[..., :D_FF], up[..., D_FF:]
        g = causal_dwconv(g, ffn_conv_w[l])
        x = x + (jax.nn.silu(g) * v) @ ffn_w_down[l]
    return x
```

```python
import functools
import math

import jax
import jax.numpy as jnp
from jax import lax
from jax.experimental import pallas as pl
from jax.experimental.pallas import tpu as pltpu

F32 = jnp.float32
BF16 = jnp.bfloat16

D_MODEL = 1024
ML_HEADS = 4
ML_DH = 96
ML_DHP = 128
ML_WP = ML_HEADS * ML_DHP
ML_W = ML_HEADS * ML_DH
ML_CONV = 4
SB_HEADS = 6
SB_DH = 64
SB_W = SB_HEADS * SB_DH
S5_W = 256
S5_GROUPS = 16
S5_CH = 16
S5_STATE = 64
S5_LANES = S5_GROUPS * S5_STATE
D_FF = 2816
EPS = 1e-6
NEG = -1e30

C_QK, C_V, C_O, C_SQ, C_SK, C_SV, C_U, C_G1, C_G2, C_END = (
    0, 1024, 1536, 2048, 2432, 2816, 3200, 3456, 3584, 3712)

VMEM_LIMIT = 56 * 1024 * 1024

IN_TM = 512
ML_TL = 512
ML_C = 128
SB_TQ = 256
SB_TK = 256
S5_TL = 512
FFN_TM = 512
FFN_FC = 256


def _const_spec(shape):
    nd = len(shape)
    return pl.BlockSpec(shape, lambda *_: (0,) * nd, pipeline_mode=pl.Buffered(1))


def _dot(a, b):
    return jnp.dot(a, b, preferred_element_type=F32)


def _dot_nt(a, b):
    return lax.dot_general(a, b, (((1,), (1,)), ((), ())), preferred_element_type=F32)


def _dot_tn(a, b):
    return lax.dot_general(a, b, (((0,), (0,)), ((), ())), preferred_element_type=F32)


def _split(x):
    hi = x.astype(BF16)
    lo = (x - hi.astype(F32)).astype(BF16)
    return hi, lo


def _log_sigmoid(x):
    return jnp.minimum(x, 0.0) - jnp.log(1.0 + jnp.exp(-jnp.abs(x)))


def _softplus(x):
    return jnp.maximum(x, 0.0) + jnp.log(1.0 + jnp.exp(-jnp.abs(x)))


def _sigmoid(x):
    return 1.0 / (1.0 + jnp.exp(-x))


def _in_proj_kernel(x_ref, g_ref, w_ref, seg_ref, qg_ref, kg_ref,
                    qk_ref, v_ref, o_ref, sq_ref, sk_ref, sv_ref, u_ref,
                    gr1_ref, gr2_ref, gc1_ref, gc2_ref):
    x = x_ref[0]
    ms = jnp.mean(x * x, axis=-1, keepdims=True)
    h = (x * lax.rsqrt(ms + EPS) * g_ref[...]).astype(BF16)

    def proj(a, b):
        return _dot(h, w_ref[:, a:b])

    def qk_norm(z, gain):
        ss = _dot((z * z).astype(BF16), seg_ref[...])
        return (z * lax.rsqrt(ss * (1.0 / SB_DH) + EPS) * gain).astype(BF16)

    qk_ref[0] = proj(C_QK, C_V)
    v_ref[0] = proj(C_V, C_O).astype(BF16)
    o_ref[0] = proj(C_O, C_SQ).astype(BF16)
    sq_ref[0] = qk_norm(proj(C_SQ, C_SK), qg_ref[...])
    sk_ref[0] = qk_norm(proj(C_SK, C_SV), kg_ref[...])
    sv_ref[0] = proj(C_SV, C_U).astype(BF16)
    u_ref[0] = proj(C_U, C_G1)
    z1 = proj(C_G1, C_G2)
    z2 = proj(C_G2, C_END)
    gc1_ref[0] = z1[:, :8]
    gc2_ref[0] = z2[:, :8]
    gr1_ref[0] = z1.T[:8, :]
    gr2_ref[0] = z2.T[:8, :]


def _in_proj(x, g, w, seg, qg, kg):
    B, L, _ = x.shape
    tm = min(IN_TM, L)
    grid = (B, L // tm)
    tok = lambda w_, dt: jax.ShapeDtypeStruct((B, L, w_), dt)
    tspec = lambda w_: pl.BlockSpec((1, tm, w_), lambda b, i: (b, i, 0))
    rspec = pl.BlockSpec((1, 8, tm), lambda b, i: (b, 0, i))
    return pl.pallas_call(
        _in_proj_kernel,
        grid=grid,
        in_specs=[tspec(D_MODEL), _const_spec((1, D_MODEL)), _const_spec((D_MODEL, C_END)),
                  _const_spec((SB_W, SB_W)), _const_spec((1, SB_W)), _const_spec((1, SB_W))],
        out_specs=[tspec(2 * ML_WP), tspec(ML_WP), tspec(ML_WP), tspec(SB_W), tspec(SB_W), tspec(SB_W),
                   tspec(S5_W), rspec, rspec, tspec(8), tspec(8)],
        out_shape=[tok(2 * ML_WP, F32), tok(ML_WP, BF16), tok(ML_WP, BF16), tok(SB_W, BF16),
                   tok(SB_W, BF16), tok(SB_W, BF16), tok(S5_W, F32),
                   jax.ShapeDtypeStruct((B, 8, L), F32), jax.ShapeDtypeStruct((B, 8, L), F32),
                   tok(8, F32), tok(8, F32)],
        compiler_params=pltpu.CompilerParams(
            dimension_semantics=("parallel", "parallel"), vmem_limit_bytes=VMEM_LIMIT),
        name="in_proj",
    )(x, g, w, seg, qg, kg)


def _mlstm_kernel(qk_ref, v_ref, o_ref, gr1_ref, gr2_ref, gc1_ref, gc2_ref,
                  cw_ref, cb_ref, gbr1_ref, gbr2_ref, gbc1_ref, gbc2_ref, og_ref,
                  out_ref, ext_s, st_s, m_s, *, tl):
    l = pl.program_id(1)

    @pl.when(l == 0)
    def _():
        ext_s[0:8, :] = jnp.zeros((8, 2 * ML_WP), F32)
        st_s[...] = jnp.zeros_like(st_s)
        m_s[...] = jnp.zeros_like(m_s)

    pre = qk_ref[0]
    ext_s[8:8 + tl, :] = pre
    conv = (cw_ref[3:4, :] * pre + cw_ref[2:3, :] * ext_s[7:7 + tl, :]
            + cw_ref[1:2, :] * ext_s[6:6 + tl, :] + cw_ref[0:1, :] * ext_s[5:5 + tl, :] + cb_ref[...])
    ext_s[0:8, :] = pre[tl - 8:tl, :]
    qk = conv * _sigmoid(conv)
    q_all = qk[:, :ML_WP].astype(BF16)
    k_all = qk[:, ML_WP:] * (ML_DH ** -0.5)

    gr1 = gr1_ref[0] + gbr1_ref[...]
    gr2 = gr2_ref[0] + gbr2_ref[...]
    gc1 = gc1_ref[0] + gbc1_ref[...]
    gc2 = gc2_ref[0] + gbc2_ref[...]

    ri = lax.broadcasted_iota(jnp.int32, (ML_C, ML_C), 0)
    ci = lax.broadcasted_iota(jnp.int32, (ML_C, ML_C), 1)
    tril = ci <= ri
    tri_low = jnp.where(tril, 1.0, 0.0).astype(BF16)
    tri_up = jnp.where(ri <= ci, 1.0, 0.0).astype(BF16)
    lane = lax.broadcasted_iota(jnp.int32, (ML_C, ML_DHP), 1)

    for c in range(tl // ML_C):
        rows = slice(c * ML_C, (c + 1) * ML_C)
        ic = gc1[rows, :]
        lf_hi, lf_lo = _split(_log_sigmoid(gc2[rows, :]))
        b_col = _dot(tri_low, lf_hi) + _dot(tri_low, lf_lo)
        ir = gr1[:, rows]
        lr_hi, lr_lo = _split(_log_sigmoid(gr2[:, rows]))
        b_row = _dot(lr_hi, tri_up) + _dot(lr_lo, tri_up)
        r_row = ir - b_row
        btot = b_col[ML_C - 1:ML_C, :]
        mp = m_s[...]
        wlog = btot - b_col + ic
        m_loc = jnp.max(wlog, axis=0, keepdims=True)
        wexp = jnp.exp(wlog - m_loc)
        m_new = jnp.maximum(btot + mp, m_loc)
        a_sc = jnp.exp(btot + mp - m_new)
        c_sc = jnp.exp(m_loc - m_new)
        inter_log = b_col + mp

        for hd in range(ML_HEADS):
            cols = slice(hd * ML_DHP, (hd + 1) * ML_DHP)
            dlog = jnp.where(tril, b_col[:, hd:hd + 1] + r_row[hd:hd + 1, :], NEG)
            m_t = jnp.maximum(inter_log[:, hd:hd + 1], jnp.max(dlog, axis=-1, keepdims=True))
            dmat = jnp.exp(dlog - m_t)
            qh = q_all[rows, cols]
            kh = k_all[rows, cols]
            s = _dot_nt(qh, kh.astype(BF16))
            p = (s * dmat).astype(BF16)
            v_aug = jnp.where(lane == ML_DH, 1.0, v_ref[0, rows, cols].astype(F32)).astype(BF16)
            st = st_s[hd]
            isc = jnp.exp(inter_log[:, hd:hd + 1] - m_t)
            acc = _dot(p, v_aug) + isc * _dot(qh, st.astype(BF16))
            den = acc[:, ML_DH:ML_DH + 1]
            hv = acc * (1.0 / jnp.maximum(jnp.abs(den), jnp.exp(-m_t)))
            hv = jnp.where(lane < ML_DH, hv, 0.0)
            ss = jnp.sum(hv * hv, axis=-1, keepdims=True) * (1.0 / ML_DH)
            hn = hv * lax.rsqrt(ss + EPS) * og_ref[:, cols] * _sigmoid(o_ref[0, rows, cols].astype(F32))
            out_ref[0, rows, cols] = hn.astype(BF16)
            kw = (kh * wexp[:, hd:hd + 1]).astype(BF16)
            st_s[hd] = a_sc[:, hd:hd + 1] * st + c_sc[:, hd:hd + 1] * _dot_tn(kw, v_aug)
        m_s[...] = m_new


def _mlstm(qk, v, o, gr1, gr2, gc1, gc2, cw, cb, gbr1, gbr2, gbc1, gbc2, og):
    B, L, _ = qk.shape
    tl = min(ML_TL, L)
    grid = (B, L // tl)
    tspec = lambda w_: pl.BlockSpec((1, tl, w_), lambda b, i: (b, i, 0))
    rspec = pl.BlockSpec((1, 8, tl), lambda b, i: (b, 0, i))
    return pl.pallas_call(
        functools.partial(_mlstm_kernel, tl=tl),
        grid=grid,
        in_specs=[tspec(2 * ML_WP), tspec(ML_WP), tspec(ML_WP), rspec, rspec, tspec(8), tspec(8),
                  _const_spec((ML_CONV, 2 * ML_WP)), _const_spec((1, 2 * ML_WP)),
                  _const_spec((8, 1)), _const_spec((8, 1)), _const_spec((1, 8)), _const_spec((1, 8)),
                  _const_spec((1, ML_WP))],
        out_specs=tspec(ML_WP),
        out_shape=jax.ShapeDtypeStruct((B, L, ML_WP), BF16),
        scratch_shapes=[pltpu.VMEM((tl + 8, 2 * ML_WP), F32),
                        pltpu.VMEM((ML_HEADS, ML_DHP, ML_DHP), F32),
                        pltpu.VMEM((1, 8), F32)],
        compiler_params=pltpu.CompilerParams(
            dimension_semantics=("parallel", "arbitrary"), vmem_limit_bytes=VMEM_LIMIT),
        name="mlstm",
    )(qk, v, o, gr1, gr2, gc1, gc2, cw, cb, gbr1, gbr2, gbc1, gbc2, og)


def _sb_kernel(q_ref, k_ref, v_ref, og_ref, out_ref, acc_s, r_s):
    i = pl.program_id(2)
    q = q_ref[0]
    lane = lax.broadcasted_iota(jnp.int32, (SB_TQ, 2 * SB_DH), 1)
    first = lane < SB_DH
    zero = jnp.zeros_like(q)
    q_heads = (jnp.where(first, q, zero), jnp.where(first, zero, q))
    ri = lax.broadcasted_iota(jnp.int32, (SB_TQ, SB_TK), 0)
    ci = lax.broadcasted_iota(jnp.int32, (SB_TQ, SB_TK), 1)
    causal = ci < ri
    rj = lax.broadcasted_iota(jnp.int32, (SB_TK, SB_TK), 0)
    cj = lax.broadcasted_iota(jnp.int32, (SB_TK, SB_TK), 1)
    suffix = jnp.where(rj >= cj, 1.0, 0.0).astype(BF16)

    acc_s[...] = jnp.zeros_like(acc_s)
    r_s[...] = jnp.zeros_like(r_s)

    def block(j, masked):
        start = pl.multiple_of(j * SB_TK, SB_TK)
        k = k_ref[0, pl.ds(start, SB_TK), :]
        v = v_ref[0, pl.ds(start, SB_TK), :]
        for hd in range(2):
            z = _dot_nt(q_heads[hd], k)
            sp = _softplus(z)
            if masked:
                sp = jnp.where(causal, sp, 0.0)
            hi, lo = _split(sp)
            sb = _dot(hi, suffix) + _dot(lo, suffix)
            r = r_s[hd]
            a = jnp.exp(z - (sb + r))
            if masked:
                a = jnp.where(causal, a, 0.0)
            acc_s[hd] += _dot(a.astype(BF16), v)
            r_s[hd] = r + sb[:, 0:1]

    block(i, True)

    def body(t, carry):
        block(i - 1 - t, False)
        return carry

    lax.fori_loop(0, i, body, 0)

    o = jnp.where(first, acc_s[0], acc_s[1])
    o2 = o * o
    ss_a = jnp.sum(jnp.where(first, o2, 0.0), axis=-1, keepdims=True)
    ss_b = jnp.sum(jnp.where(first, 0.0, o2), axis=-1, keepdims=True)
    ss = jnp.where(first, ss_a, ss_b) * (1.0 / SB_DH)
    out_ref[0] = (o * lax.rsqrt(ss + EPS) * og_ref[0]).astype(BF16)


def _sb_attn(q, k, v, og):
    B, L, _ = q.shape
    assert L % SB_TQ == 0 and SB_TQ == SB_TK
    npair = SB_HEADS // 2
    grid = (B, npair, L // SB_TQ)
    return pl.pallas_call(
        _sb_kernel,
        grid=grid,
        in_specs=[pl.BlockSpec((1, SB_TQ, 2 * SB_DH), lambda b, p, i: (b, i, p)),
                  pl.BlockSpec((1, L, 2 * SB_DH), lambda b, p, i: (b, 0, p)),
                  pl.BlockSpec((1, L, 2 * SB_DH), lambda b, p, i: (b, 0, p)),
                  pl.BlockSpec((1, 1, 2 * SB_DH), lambda b, p, i: (p, 0, 0))],
        out_specs=pl.BlockSpec((1, SB_TQ, 2 * SB_DH), lambda b, p, i: (b, i, p)),
        out_shape=jax.ShapeDtypeStruct((B, L, SB_W), BF16),
        scratch_shapes=[pltpu.VMEM((2, SB_TQ, 2 * SB_DH), F32),
                        pltpu.VMEM((2, SB_TQ, 1), F32)],
        compiler_params=pltpu.CompilerParams(
            dimension_semantics=("parallel", "parallel", "parallel"), vmem_limit_bytes=VMEM_LIMIT),
        name="sb_attn",
    )(q, k, v, og)


def _gelu_tanh(x):
    return 0.5 * x * (1.0 + jnp.tanh(math.sqrt(2.0 / math.pi) * (x + 0.044715 * (x * x * x))))


def _s5_kernel(u_ref, bm_ref, cm_ref, pw_ref, d_ref, gw_ref, gb_ref, og_ref, out_ref, xs_s, carry_s, *, tl):
    l = pl.program_id(1)

    @pl.when(l == 0)
    def _():
        carry_s[...] = jnp.zeros_like(carry_s)

    u = u_ref[0]
    xs_s[...] = _dot(u.astype(BF16), bm_ref[...])

    nlt = S5_LANES // 128

    def row_tile(r, carry):
        r0 = pl.multiple_of(r * 8, 8)
        new = []
        for lt in range(nlt):
            re_l = slice(lt * 128, (lt + 1) * 128)
            im_l = slice(S5_LANES + lt * 128, S5_LANES + (lt + 1) * 128)
            xr = xs_s[pl.ds(r0, 8), re_l]
            xi = xs_s[pl.ds(r0, 8), im_l]
            for n, k in enumerate((1, 2, 4)):
                mr = pw_ref[2 * n, :, re_l]
                mi = pw_ref[2 * n + 1, :, re_l]
                sr = pltpu.roll(xr, k, 0)
                si = pltpu.roll(xi, k, 0)
                xr, xi = xr + mr * sr - mi * si, xi + mr * si + mi * sr
            cr, ci = carry[2 * lt], carry[2 * lt + 1]
            pr = pw_ref[6, :, re_l]
            pi_ = pw_ref[7, :, re_l]
            xr, xi = xr + pr * cr - pi_ * ci, xi + pr * ci + pi_ * cr
            xs_s[pl.ds(r0, 8), re_l] = xr
            xs_s[pl.ds(r0, 8), im_l] = xi
            new.append(jnp.broadcast_to(xr[7:8, :], (8, 128)))
            new.append(jnp.broadcast_to(xi[7:8, :], (8, 128)))
        return tuple(new)

    carry0 = tuple(carry_s[:, j * 128:(j + 1) * 128] for j in range(2 * nlt))
    carry = lax.fori_loop(0, tl // 8, row_tile, carry0)
    for j in range(2 * nlt):
        carry_s[:, j * 128:(j + 1) * 128] = carry[j]

    y = _dot(xs_s[...].astype(BF16), cm_ref[...]) + d_ref[...] * u
    y = _gelu_tanh(y)
    y = y * _sigmoid(_dot(y.astype(BF16), gw_ref[...]) + gb_ref[...])
    ms = jnp.mean(y * y, axis=-1, keepdims=True)
    out_ref[0] = (y * lax.rsqrt(ms + EPS) * og_ref[...]).astype(BF16)


def _s5(u, bm, cm, pw, d, gw, gb, og):
    B, L, _ = u.shape
    tl = min(S5_TL, L)
    grid = (B, L // tl)
    tspec = pl.BlockSpec((1, tl, S5_W), lambda b, i: (b, i, 0))
    return pl.pallas_call(
        functools.partial(_s5_kernel, tl=tl),
        grid=grid,
        in_specs=[tspec, _const_spec((S5_W, 2 * S5_LANES)), _const_spec((2 * S5_LANES, S5_W)),
                  _const_spec((8, 8, S5_LANES)), _const_spec((1, S5_W)), _const_spec((S5_W, S5_W)),
                  _const_spec((1, S5_W)), _const_spec((1, S5_W))],
        out_specs=tspec,
        out_shape=jax.ShapeDtypeStruct((B, L, S5_W), BF16),
        scratch_shapes=[pltpu.VMEM((tl, 2 * S5_LANES), F32), pltpu.VMEM((8, 2 * S5_LANES), F32)],
        compiler_params=pltpu.CompilerParams(
            dimension_semantics=("parallel", "arbitrary"), vmem_limit_bytes=VMEM_LIMIT),
        name="s5",
    )(u, bm, cm, pw, d, gw, gb, og)


def _ffn_kernel(x_ref, hm_ref, hs_ref, h5_ref, wom_ref, wos_ref, wo5_ref, g_ref,
                wg_ref, wv_ref, cw_ref, wd_ref, out_ref, h_s, gext_s, act_s, carry_s, *, tm):
    l = pl.program_id(1)

    @pl.when(l == 0)
    def _():
        carry_s[...] = jnp.zeros_like(carry_s)

    x1 = (x_ref[0] + _dot(hm_ref[0], wom_ref[...]) + _dot(hs_ref[0], wos_ref[...])
          + _dot(h5_ref[0], wo5_ref[...]))
    out_ref[0] = x1
    ms = jnp.mean(x1 * x1, axis=-1, keepdims=True)
    h_s[...] = (x1 * lax.rsqrt(ms + EPS) * g_ref[...]).astype(BF16)

    for c in range(D_FF // FFN_FC):
        cols = slice(c * FFN_FC, (c + 1) * FFN_FC)
        g = _dot(h_s[...], wg_ref[:, cols])
        v = _dot(h_s[...], wv_ref[:, cols])
        gext_s[0:8, :] = carry_s[c]
        gext_s[8:8 + tm, :] = g
        carry_s[c] = g[tm - 8:tm, :]
        conv = (cw_ref[2:3, cols] * g + cw_ref[1:2, cols] * gext_s[7:7 + tm, :]
                + cw_ref[0:1, cols] * gext_s[6:6 + tm, :])
        act_s[:, cols] = (conv * _sigmoid(conv) * v).astype(BF16)

    out_ref[0] += _dot(act_s[...], wd_ref[...])


def _ffn(x, hm, hs, h5, wom, wos, wo5, g, wg, wv, cw, wd):
    B, L, _ = x.shape
    tm = min(FFN_TM, L)
    grid = (B, L // tm)
    tspec = lambda w_: pl.BlockSpec((1, tm, w_), lambda b, i: (b, i, 0))
    return pl.pallas_call(
        functools.partial(_ffn_kernel, tm=tm),
        grid=grid,
        in_specs=[tspec(D_MODEL), tspec(ML_WP), tspec(SB_W), tspec(S5_W),
                  _const_spec((ML_WP, D_MODEL)), _const_spec((SB_W, D_MODEL)), _const_spec((S5_W, D_MODEL)),
                  _const_spec((1, D_MODEL)), _const_spec((D_MODEL, D_FF)), _const_spec((D_MODEL, D_FF)),
                  _const_spec((3, D_FF)), _const_spec((D_FF, D_MODEL))],
        out_specs=tspec(D_MODEL),
        out_shape=jax.ShapeDtypeStruct((B, L, D_MODEL), F32),
        scratch_shapes=[pltpu.VMEM((tm, D_MODEL), BF16), pltpu.VMEM((tm + 8, FFN_FC), F32),
                        pltpu.VMEM((tm, D_FF), BF16), pltpu.VMEM((D_FF // FFN_FC, 8, FFN_FC), F32)],
        compiler_params=pltpu.CompilerParams(
            dimension_semantics=("parallel", "arbitrary"), vmem_limit_bytes=VMEM_LIMIT),
        name="ffn",
    )(x, hm, hs, h5, wom, wos, wo5, g, wg, wv, cw, wd)


def _pad_heads(w):
    lead = w.shape[:-1]
    w = w.reshape(lead + (ML_HEADS, ML_DH))
    w = jnp.pad(w, [(0, 0)] * len(lead) + [(0, 0), (0, ML_DHP - ML_DH)])
    return w.reshape(lead + (ML_WP,))


def _pad_lanes(w, n):
    return jnp.pad(w, [(0, 0)] * (w.ndim - 1) + [(0, n - w.shape[-1])])


def _s5_params(a_re, a_im, log_dt, b_re, b_im, c_re, c_im):
    lr = jnp.minimum(a_re, -1e-4)
    li = a_im
    dt = jnp.exp(log_dt)[:, None]
    mag = jnp.exp(lr * dt)
    br = mag * jnp.cos(li * dt)
    bi = mag * jnp.sin(li * dt)
    nr, ni = br - 1.0, bi
    den = lr * lr + li * li
    fr = (nr * lr + ni * li) / den
    fi = (ni * lr - nr * li) / den
    bbr = fr[..., None] * b_re - fi[..., None] * b_im
    bbi = fr[..., None] * b_im + fi[..., None] * b_re
    eye = jnp.eye(S5_GROUPS, dtype=F32)
    bm_re = jnp.einsum("gk,gph->ghkp", eye, bbr).reshape(S5_W, S5_LANES)
    bm_im = jnp.einsum("gk,gph->ghkp", eye, bbi).reshape(S5_W, S5_LANES)
    bm = jnp.concatenate([bm_re, bm_im], axis=1).astype(BF16)
    cm_re = jnp.einsum("gk,ghp->gpkh", eye, c_re).reshape(S5_LANES, S5_W)
    cm_im = jnp.einsum("gk,ghp->gpkh", eye, c_im).reshape(S5_LANES, S5_W)
    cm = jnp.concatenate([cm_re, -cm_im], axis=0).astype(BF16)
    pr, pi_ = [br.reshape(-1)], [bi.reshape(-1)]
    for _ in range(7):
        r, i = pr[-1], pi_[-1]
        pr.append(r * pr[0] - i * pi_[0])
        pi_.append(r * pi_[0] + i * pr[0])
    row = jnp.arange(8)[:, None]
    planes = []
    for k in (1, 2, 4):
        keep = row >= k
        planes.append(jnp.where(keep, pr[k - 1][None, :], 0.0))
        planes.append(jnp.where(keep, pi_[k - 1][None, :], 0.0))
    planes.append(jnp.stack(pr, axis=0))
    planes.append(jnp.stack(pi_, axis=0))
    pw = jnp.stack(planes, axis=0).astype(F32)
    return bm, cm, pw


def kernel(x, norm_mix_g, w_in, ml_conv_w, ml_conv_b, ml_gate_b, ml_out_g, sb_q_g, sb_k_g, sb_out_g, s5_a_re, s5_a_im, s5_log_dt, s5_b_re, s5_b_im, s5_c_re, s5_c_im, s5_d, s5_glu_w, s5_glu_b, s5_out_g, w_out, norm_ffn_g, ffn_w_up, ffn_conv_w, ffn_w_down):
    depth = w_in.shape[0]
    seg = jnp.kron(jnp.eye(SB_HEADS, dtype=F32), jnp.ones((SB_DH, SB_DH), F32)).astype(BF16)
    for l in range(depth):
        w = w_in[l]
        o0 = 0
        parts = []
        for wd_ in (ML_W, ML_W, ML_W, ML_W, 2 * ML_HEADS, SB_W, SB_W, SB_W, S5_W):
            parts.append(w[:, o0:o0 + wd_])
            o0 += wd_
        wq, wk, wv_, wo_, wif, wsq, wsk, wsv, wu = parts
        wfi = jnp.concatenate([wif[:, ML_HEADS:], wif[:, :ML_HEADS]], axis=1)
        w_all = jnp.concatenate(
            [_pad_heads(wq), _pad_heads(wk), _pad_heads(wv_), _pad_heads(wo_), wsq, wsk, wsv, wu,
             _pad_lanes(wif, 128), _pad_lanes(wfi, 128)], axis=1).astype(BF16)
        qg = jnp.tile(sb_q_g[l], SB_HEADS)[None, :] * (SB_DH ** -0.5)
        kg = jnp.tile(sb_k_g[l], SB_HEADS)[None, :]
        (qk, mv, mo, sq, sk, sv, u, gr1, gr2, gc1, gc2) = _in_proj(
            x, norm_mix_g[l][None, :], w_all, seg, qg, kg)

        cw = jnp.concatenate([_pad_heads(ml_conv_w[l][:, :ML_W]), _pad_heads(ml_conv_w[l][:, ML_W:])], axis=1)
        cb = jnp.concatenate([_pad_heads(ml_conv_b[l][:ML_W]), _pad_heads(ml_conv_b[l][ML_W:])])[None, :]
        gb = ml_gate_b[l]
        gb2 = jnp.concatenate([gb[ML_HEADS:], gb[:ML_HEADS]])
        hm = _mlstm(qk, mv, mo, gr1, gr2, gc1, gc2, cw, cb, gb[:, None], gb2[:, None], gb[None, :],
                    gb2[None, :], _pad_heads(ml_out_g[l].reshape(-1))[None, :])

        hs = _sb_attn(sq, sk, sv, sb_out_g[l].reshape(SB_HEADS // 2, 1, 2 * SB_DH))

        bm, cm, pw = _s5_params(s5_a_re[l], s5_a_im[l], s5_log_dt[l], s5_b_re[l], s5_b_im[l],
                                s5_c_re[l], s5_c_im[l])
        h5 = _s5(u, bm, cm, pw, s5_d[l].reshape(1, S5_W), s5_glu_w[l].astype(BF16),
                 s5_glu_b[l][None, :], s5_out_g[l][None, :])

        wo = w_out[l]
        wom = jnp.pad(wo[:ML_W].reshape(ML_HEADS, ML_DH, D_MODEL),
                      [(0, 0), (0, ML_DHP - ML_DH), (0, 0)]).reshape(ML_WP, D_MODEL).astype(BF16)
        wos = wo[ML_W:ML_W + SB_W].astype(BF16)
        wo5 = wo[ML_W + SB_W:].astype(BF16)
        x = _ffn(x, hm, hs, h5, wom, wos, wo5, norm_ffn_g[l][None, :],
                 ffn_w_up[l][:, :D_FF].astype(BF16), ffn_w_up[l][:, D_FF:].astype(BF16),
                 ffn_conv_w[l], ffn_w_down[l].astype(BF16))
    return x
```

```python
import functools
import math

import jax
import jax.numpy as jnp
from jax import lax
from jax.experimental import pallas as pl
from jax.experimental.pallas import tpu as pltpu

F32 = jnp.float32
BF16 = jnp.bfloat16

D_MODEL = 1024
ML_HEADS = 4
ML_DH = 96
ML_DHP = 128
ML_WP = ML_HEADS * ML_DHP
ML_W = ML_HEADS * ML_DH
ML_CONV = 4
SB_HEADS = 6
SB_DH = 64
SB_W = SB_HEADS * SB_DH
S5_W = 256
S5_GROUPS = 16
S5_CH = 16
S5_STATE = 64
S5_LANES = S5_GROUPS * S5_STATE
D_FF = 2816
EPS = 1e-6
NEG = -1e30

C_QK, C_V, C_O, C_SQ, C_SK, C_SV, C_U, C_G1, C_G2, C_END = (
    0, 1024, 1536, 2048, 2432, 2816, 3200, 3456, 3584, 3712)

VMEM_LIMIT = 56 * 1024 * 1024

IN_TM = 512
ML_TL = 512
ML_C = 128
SB_TQ = 512
SB_TK = 256
S5_TL = 512
FFN_TM = 512
FFN_FC = 256


def _const_spec(shape):
    nd = len(shape)
    return pl.BlockSpec(shape, lambda *_: (0,) * nd, pipeline_mode=pl.Buffered(1))


def _dot(a, b):
    return jnp.dot(a, b, preferred_element_type=F32)


def _dot_nt(a, b):
    return lax.dot_general(a, b, (((1,), (1,)), ((), ())), preferred_element_type=F32)


def _dot_tn(a, b):
    return lax.dot_general(a, b, (((0,), (0,)), ((), ())), preferred_element_type=F32)


def _split(x):
    hi = x.astype(BF16)
    lo = (x - hi.astype(F32)).astype(BF16)
    return hi, lo


def _log_sigmoid(x):
    return jnp.minimum(x, 0.0) - jnp.log(1.0 + jnp.exp(-jnp.abs(x)))


def _softplus(x):
    return jnp.maximum(x, 0.0) + jnp.log(1.0 + jnp.exp(-jnp.abs(x)))


def _sigmoid(x):
    return 1.0 / (1.0 + jnp.exp(-x))


def _in_proj_kernel(x_ref, g_ref, w_ref, seg_ref, qg_ref, kg_ref,
                    qk_ref, v_ref, o_ref, sq_ref, sk_ref, sv_ref, u_ref,
                    gr1_ref, gr2_ref, gc1_ref, gc2_ref):
    x = x_ref[0]
    ms = jnp.mean(x * x, axis=-1, keepdims=True)
    h = (x * lax.rsqrt(ms + EPS) * g_ref[...]).astype(BF16)

    def proj(a, b):
        return _dot(h, w_ref[:, a:b])

    def qk_norm(z, gain):
        ss = _dot((z * z).astype(BF16), seg_ref[...])
        return (z * lax.rsqrt(ss * (1.0 / SB_DH) + EPS) * gain).astype(BF16)

    qk_ref[0] = proj(C_QK, C_V)
    v_ref[0] = proj(C_V, C_O).astype(BF16)
    o_ref[0] = proj(C_O, C_SQ).astype(BF16)
    sq_ref[0] = qk_norm(proj(C_SQ, C_SK), qg_ref[...])
    sk_ref[0] = qk_norm(proj(C_SK, C_SV), kg_ref[...])
    sv_ref[0] = proj(C_SV, C_U).astype(BF16)
    zu = proj(C_U, C_G1)
    nseg = zu.shape[0] // 8
    for r in range(8):
        for hf in range(S5_W // 128):
            u_ref[0, hf, pl.ds(r, nseg, stride=8), :] = zu[r * nseg:(r + 1) * nseg, hf * 128:(hf + 1) * 128]
    z1 = proj(C_G1, C_G2)
    z2 = proj(C_G2, C_END)
    gc1_ref[0] = z1[:, :8]
    gc2_ref[0] = z2[:, :8]
    gr1_ref[0] = z1.T[:8, :]
    gr2_ref[0] = z2.T[:8, :]


def _in_proj(x, g, w, seg, qg, kg):
    B, L, _ = x.shape
    tm = min(IN_TM, L)
    grid = (B, L // tm)
    tok = lambda w_, dt: jax.ShapeDtypeStruct((B, L, w_), dt)
    tspec = lambda w_: pl.BlockSpec((1, tm, w_), lambda b, i: (b, i, 0))
    rspec = pl.BlockSpec((1, 8, tm), lambda b, i: (b, 0, i))
    return pl.pallas_call(
        _in_proj_kernel,
        grid=grid,
        in_specs=[tspec(D_MODEL), _const_spec((1, D_MODEL)), _const_spec((D_MODEL, C_END)),
                  _const_spec((SB_W, SB_W)), _const_spec((1, SB_W)), _const_spec((1, SB_W))],
        out_specs=[tspec(2 * ML_WP), tspec(ML_WP), tspec(ML_WP), tspec(SB_W), tspec(SB_W), tspec(SB_W),
                   pl.BlockSpec((1, S5_W // 128, tm, 128), lambda b, i: (b, 0, i, 0)),
                   rspec, rspec, tspec(8), tspec(8)],
        out_shape=[tok(2 * ML_WP, F32), tok(ML_WP, BF16), tok(ML_WP, BF16), tok(SB_W, BF16),
                   tok(SB_W, BF16), tok(SB_W, BF16), jax.ShapeDtypeStruct((B, S5_W // 128, L, 128), F32),
                   jax.ShapeDtypeStruct((B, 8, L), F32), jax.ShapeDtypeStruct((B, 8, L), F32),
                   tok(8, F32), tok(8, F32)],
        compiler_params=pltpu.CompilerParams(
            dimension_semantics=("parallel", "parallel"), vmem_limit_bytes=VMEM_LIMIT),
        name="in_proj",
    )(x, g, w, seg, qg, kg)


def _mlstm_kernel(qk_ref, v_ref, o_ref, gr1_ref, gr2_ref, gc1_ref, gc2_ref,
                  cw_ref, cb_ref, gbr1_ref, gbr2_ref, gbc1_ref, gbc2_ref, og_ref,
                  out_ref, ext_s, st_s, m_s, *, tl):
    l = pl.program_id(1)

    @pl.when(l == 0)
    def _():
        ext_s[0:8, :] = jnp.zeros((8, 2 * ML_WP), F32)
        st_s[...] = jnp.zeros_like(st_s)
        m_s[...] = jnp.zeros_like(m_s)

    pre = qk_ref[0]
    ext_s[8:8 + tl, :] = pre
    conv = (cw_ref[3:4, :] * pre + cw_ref[2:3, :] * ext_s[7:7 + tl, :]
            + cw_ref[1:2, :] * ext_s[6:6 + tl, :] + cw_ref[0:1, :] * ext_s[5:5 + tl, :] + cb_ref[...])
    ext_s[0:8, :] = pre[tl - 8:tl, :]
    qk = conv * _sigmoid(conv)
    q_all = qk[:, :ML_WP].astype(BF16)
    k_all = qk[:, ML_WP:] * (ML_DH ** -0.5)

    gr1 = gr1_ref[0] + gbr1_ref[...]
    gr2 = gr2_ref[0] + gbr2_ref[...]
    gc1 = gc1_ref[0] + gbc1_ref[...]
    gc2 = gc2_ref[0] + gbc2_ref[...]

    ri = lax.broadcasted_iota(jnp.int32, (ML_C, ML_C), 0)
    ci = lax.broadcasted_iota(jnp.int32, (ML_C, ML_C), 1)
    tril = ci <= ri
    tri_low = jnp.where(tril, 1.0, 0.0).astype(BF16)
    tri_up = jnp.where(ri <= ci, 1.0, 0.0).astype(BF16)
    lane = lax.broadcasted_iota(jnp.int32, (ML_C, ML_DHP), 1)

    for c in range(tl // ML_C):
        rows = slice(c * ML_C, (c + 1) * ML_C)
        ic = gc1[rows, :]
        lf_hi, lf_lo = _split(_log_sigmoid(gc2[rows, :]))
        b_col = _dot(tri_low, lf_hi) + _dot(tri_low, lf_lo)
        ir = gr1[:, rows]
        lr_hi, lr_lo = _split(_log_sigmoid(gr2[:, rows]))
        b_row = _dot(lr_hi, tri_up) + _dot(lr_lo, tri_up)
        r_row = ir - b_row
        btot = b_col[ML_C - 1:ML_C, :]
        mp = m_s[...]
        wlog = btot - b_col + ic
        m_loc = jnp.max(wlog, axis=0, keepdims=True)
        wexp = jnp.exp(wlog - m_loc)
        m_new = jnp.maximum(btot + mp, m_loc)
        a_sc = jnp.exp(btot + mp - m_new)
        c_sc = jnp.exp(m_loc - m_new)
        inter_log = b_col + mp

        for hd in range(ML_HEADS):
            cols = slice(hd * ML_DHP, (hd + 1) * ML_DHP)
            dlog = jnp.where(tril, b_col[:, hd:hd + 1] + r_row[hd:hd + 1, :], NEG)
            m_t = jnp.maximum(inter_log[:, hd:hd + 1], jnp.max(dlog, axis=-1, keepdims=True))
            dmat = jnp.exp(dlog - m_t)
            qh = q_all[rows, cols]
            kh = k_all[rows, cols]
            s = _dot_nt(qh, kh.astype(BF16))
            p = (s * dmat).astype(BF16)
            v_aug = jnp.where(lane == ML_DH, 1.0, v_ref[0, rows, cols].astype(F32)).astype(BF16)
            st = st_s[hd]
            isc = jnp.exp(inter_log[:, hd:hd + 1] - m_t)
            acc = _dot(p, v_aug) + isc * _dot(qh, st.astype(BF16))
            den = acc[:, ML_DH:ML_DH + 1]
            hv = acc * (1.0 / jnp.maximum(jnp.abs(den), jnp.exp(-m_t)))
            hv = jnp.where(lane < ML_DH, hv, 0.0)
            ss = jnp.sum(hv * hv, axis=-1, keepdims=True) * (1.0 / ML_DH)
            hn = hv * lax.rsqrt(ss + EPS) * og_ref[:, cols] * _sigmoid(o_ref[0, rows, cols].astype(F32))
            out_ref[0, rows, cols] = hn.astype(BF16)
            kw = (kh * wexp[:, hd:hd + 1]).astype(BF16)
            st_s[hd] = a_sc[:, hd:hd + 1] * st + c_sc[:, hd:hd + 1] * _dot_tn(kw, v_aug)
        m_s[...] = m_new


def _mlstm(qk, v, o, gr1, gr2, gc1, gc2, cw, cb, gbr1, gbr2, gbc1, gbc2, og):
    B, L, _ = qk.shape
    tl = min(ML_TL, L)
    grid = (B, L // tl)
    tspec = lambda w_: pl.BlockSpec((1, tl, w_), lambda b, i: (b, i, 0))
    rspec = pl.BlockSpec((1, 8, tl), lambda b, i: (b, 0, i))
    return pl.pallas_call(
        functools.partial(_mlstm_kernel, tl=tl),
        grid=grid,
        in_specs=[tspec(2 * ML_WP), tspec(ML_WP), tspec(ML_WP), rspec, rspec, tspec(8), tspec(8),
                  _const_spec((ML_CONV, 2 * ML_WP)), _const_spec((1, 2 * ML_WP)),
                  _const_spec((8, 1)), _const_spec((8, 1)), _const_spec((1, 8)), _const_spec((1, 8)),
                  _const_spec((1, ML_WP))],
        out_specs=tspec(ML_WP),
        out_shape=jax.ShapeDtypeStruct((B, L, ML_WP), BF16),
        scratch_shapes=[pltpu.VMEM((tl + 8, 2 * ML_WP), F32),
                        pltpu.VMEM((ML_HEADS, ML_DHP, ML_DHP), F32),
                        pltpu.VMEM((1, 8), F32)],
        compiler_params=pltpu.CompilerParams(
            dimension_semantics=("parallel", "arbitrary"), vmem_limit_bytes=VMEM_LIMIT),
        name="mlstm",
    )(qk, v, o, gr1, gr2, gc1, gc2, cw, cb, gbr1, gbr2, gbc1, gbc2, og)


def _sb_kernel(q_ref, k_ref, v_ref, og_ref, out_ref, acc_s, r_s):
    i = pl.program_id(1)
    npair = SB_HEADS // 2
    lane = lax.broadcasted_iota(jnp.int32, (SB_TQ, 2 * SB_DH), 1)
    first = lane < SB_DH
    q_heads = []
    for p in range(npair):
        q = q_ref[0, :, p * 128:(p + 1) * 128]
        zero = jnp.zeros_like(q)
        q_heads.append(jnp.where(first, q, zero))
        q_heads.append(jnp.where(first, zero, q))
    ri = lax.broadcasted_iota(jnp.int32, (SB_TQ, SB_TK), 0)
    ci = lax.broadcasted_iota(jnp.int32, (SB_TQ, SB_TK), 1)
    rj = lax.broadcasted_iota(jnp.int32, (SB_TK, SB_TK), 0)
    cj = lax.broadcasted_iota(jnp.int32, (SB_TK, SB_TK), 1)
    suffix = jnp.where(rj >= cj, 1.0, 0.0).astype(BF16)
    sign = jnp.uint32(0x80000000)

    acc_s[...] = jnp.zeros_like(acc_s)
    r_s[...] = jnp.zeros_like(r_s)

    def blocks(tiles):
        units = [(t, hd) for t in range(len(tiles)) for hd in range(SB_HEADS)]
        ks, vs, zs, sbs = {}, {}, {}, {}
        for t, (j, _) in enumerate(tiles):
            start = pl.multiple_of(j * SB_TK, SB_TK)
            for p in range(npair):
                ks[t, p] = k_ref[0, pl.ds(start, SB_TK), p * 128:(p + 1) * 128]
                vs[t, p] = v_ref[0, pl.ds(start, SB_TK), p * 128:(p + 1) * 128]

        def scores(u):
            t, hd = u
            zs[u] = _dot_nt(q_heads[hd], ks[t, hd // 2])

        def suffix_sums(u):
            causal = tiles[u[0]][1]
            z = zs[u]
            neg_abs = lax.bitcast_convert_type(lax.bitcast_convert_type(z, jnp.uint32) | sign, F32)
            sp = jnp.maximum(z, 0.0) + jnp.log2(1.0 + jnp.exp2(neg_abs))
            if causal is not None:
                sp = jnp.where(causal, sp, 0.0)
            sbs[u] = _dot(sp.astype(BF16), suffix)

        def weights_and_values(u):
            t, hd = u
            causal = tiles[t][1]
            r = r_s[hd]
            a = jnp.exp2(zs[u] - (sbs[u] + r))
            if causal is not None:
                a = jnp.where(causal, a, 0.0)
            r_s[hd] = r + sbs[u][:, 0:1]
            acc_s[hd] += _dot(a.astype(BF16), vs[t, hd // 2])

        stages = (scores, suffix_sums, weights_and_values)
        for step in range(len(units) + len(stages) - 1):
            for lag, stage in enumerate(stages):
                if 0 <= step - lag < len(units):
                    stage(units[step - lag])

    nd = SB_TQ // SB_TK
    blocks([(i * nd + d, ci + d * SB_TK < ri) for d in reversed(range(nd))])

    def body(t, carry):
        j = (i - t) * nd - 1
        blocks([(j - d, None) for d in range(nd)])
        return carry

    lax.fori_loop(0, i, body, 0)

    for p in range(npair):
        o = jnp.where(first, acc_s[2 * p], acc_s[2 * p + 1])
        o2 = o * o
        ss_a = jnp.sum(jnp.where(first, o2, 0.0), axis=-1, keepdims=True)
        ss_b = jnp.sum(jnp.where(first, 0.0, o2), axis=-1, keepdims=True)
        ss = jnp.where(first, ss_a, ss_b) * (1.0 / SB_DH)
        out_ref[0, :, p * 128:(p + 1) * 128] = (
            o * lax.rsqrt(ss + EPS) * og_ref[:, p * 128:(p + 1) * 128]).astype(BF16)


def _sb_attn(q, k, v, og):
    B, L, _ = q.shape
    assert L % SB_TQ == 0 and SB_TQ % SB_TK == 0
    grid = (B, L // SB_TQ)
    return pl.pallas_call(
        _sb_kernel,
        grid=grid,
        in_specs=[pl.BlockSpec((1, SB_TQ, SB_W), lambda b, i: (b, i, 0)),
                  pl.BlockSpec((1, L, SB_W), lambda b, i: (b, 0, 0)),
                  pl.BlockSpec((1, L, SB_W), lambda b, i: (b, 0, 0)),
                  _const_spec((1, SB_W))],
        out_specs=pl.BlockSpec((1, SB_TQ, SB_W), lambda b, i: (b, i, 0)),
        out_shape=jax.ShapeDtypeStruct((B, L, SB_W), BF16),
        scratch_shapes=[pltpu.VMEM((SB_HEADS, SB_TQ, 2 * SB_DH), F32),
                        pltpu.VMEM((SB_HEADS, SB_TQ, 1), F32)],
        compiler_params=pltpu.CompilerParams(
            dimension_semantics=("parallel", "parallel"), vmem_limit_bytes=VMEM_LIMIT),
        name="sb_attn",
    )(q, k, v, og)


def _gelu_tanh(x):
    return 0.5 * x * (1.0 + jnp.tanh(math.sqrt(2.0 / math.pi) * (x + 0.044715 * (x * x * x))))


def _cmul(ar, ai, br, bi):
    return ar * br - ai * bi, ar * bi + ai * br


def _s5_kernel(u_ref, bm_ref, cm_ref, pw_ref, d_ref, gw_ref, gb_ref, og_ref, out_ref, xs_s, y_s, carry_s, *, tl):
    l = pl.program_id(1)
    nseg = tl // 8
    nlt = S5_LANES // 128
    grp = 8

    @pl.when(l == 0)
    def _():
        carry_s[...] = jnp.zeros_like(carry_s)

    u = jnp.concatenate([u_ref[0, hf] for hf in range(S5_W // 128)], axis=-1)
    xs_s[...] = _dot(u.astype(BF16), bm_ref[...])

    def lanes(lt):
        return slice(lt * 128, (lt + 1) * 128), slice(S5_LANES + lt * 128, S5_LANES + (lt + 1) * 128)

    for g0 in range(0, nlt, grp):
        tiles = range(g0, g0 + grp)
        lam = [(pw_ref[0, :, lanes(lt)[0]], pw_ref[1, :, lanes(lt)[0]]) for lt in tiles]

        def local_step(j, st):
            r0 = pl.multiple_of(j * 8, 8)
            new = []
            for n, lt in enumerate(tiles):
                re_l, im_l = lanes(lt)
                pr, pi_ = _cmul(lam[n][0], lam[n][1], st[2 * n], st[2 * n + 1])
                xr = xs_s[pl.ds(r0, 8), re_l] + pr
                xi = xs_s[pl.ds(r0, 8), im_l] + pi_
                xs_s[pl.ds(r0, 8), re_l] = xr
                xs_s[pl.ds(r0, 8), im_l] = xi
                new += [xr, xi]
            return tuple(new)

        fin = lax.fori_loop(0, nseg, local_step, tuple(jnp.zeros((8, 128), F32) for _ in range(2 * grp)))

        start = []
        for n, lt in enumerate(tiles):
            re_l, im_l = lanes(lt)
            er, ei = fin[2 * n], fin[2 * n + 1]
            for m, k in enumerate((1, 2, 4)):
                pr, pi_ = _cmul(pw_ref[2 + 2 * m, :, re_l], pw_ref[3 + 2 * m, :, re_l],
                                pltpu.roll(er, k, 0), pltpu.roll(ei, k, 0))
                er, ei = er + pr, ei + pi_
            s0r, s0i = carry_s[:, re_l], carry_s[:, im_l]
            pr, pi_ = _cmul(pw_ref[8, :, re_l], pw_ref[9, :, re_l], s0r, s0i)
            er, ei = er + pr, ei + pi_
            carry_s[:, re_l] = jnp.broadcast_to(er[7:8, :], (8, 128))
            carry_s[:, im_l] = jnp.broadcast_to(ei[7:8, :], (8, 128))
            row = lax.broadcasted_iota(jnp.int32, (8, 128), 0)
            start += [jnp.where(row == 0, s0r, pltpu.roll(er, 1, 0)),
                      jnp.where(row == 0, s0i, pltpu.roll(ei, 1, 0))]

        def fix_step(j, tr):
            r0 = pl.multiple_of(j * 8, 8)
            new = []
            for n, lt in enumerate(tiles):
                re_l, im_l = lanes(lt)
                pr, pi_ = _cmul(lam[n][0], lam[n][1], tr[2 * n], tr[2 * n + 1])
                xs_s[pl.ds(r0, 8), re_l] += pr
                xs_s[pl.ds(r0, 8), im_l] += pi_
                new += [pr, pi_]
            return tuple(new)

        lax.fori_loop(0, nseg, fix_step, tuple(start))

    y = _dot(xs_s[...].astype(BF16), cm_ref[...]) + d_ref[...] * u
    y = _gelu_tanh(y)
    y = y * _sigmoid(_dot(y.astype(BF16), gw_ref[...]) + gb_ref[...])
    ms = jnp.mean(y * y, axis=-1, keepdims=True)
    y = y * lax.rsqrt(ms + EPS) * og_ref[...]
    for hf in range(S5_W // 128):
        y_s[hf] = y[:, hf * 128:(hf + 1) * 128]
    for r in range(8):
        out_ref[0, r * nseg:(r + 1) * nseg, :] = jnp.concatenate(
            [y_s[hf, pl.ds(r, nseg, stride=8), :] for hf in range(S5_W // 128)], axis=-1).astype(BF16)


def _s5(u, bm, cm, pw, d, gw, gb, og):
    B, _, L, _ = u.shape
    tl = min(S5_TL, L)
    grid = (B, L // tl)
    tspec = pl.BlockSpec((1, tl, S5_W), lambda b, i: (b, i, 0))
    uspec = pl.BlockSpec((1, S5_W // 128, tl, 128), lambda b, i: (b, 0, i, 0))
    return pl.pallas_call(
        functools.partial(_s5_kernel, tl=tl),
        grid=grid,
        in_specs=[uspec, _const_spec((S5_W, 2 * S5_LANES)), _const_spec((2 * S5_LANES, S5_W)),
                  _const_spec((10, 8, S5_LANES)), _const_spec((1, S5_W)), _const_spec((S5_W, S5_W)),
                  _const_spec((1, S5_W)), _const_spec((1, S5_W))],
        out_specs=tspec,
        out_shape=jax.ShapeDtypeStruct((B, L, S5_W), BF16),
        scratch_shapes=[pltpu.VMEM((tl, 2 * S5_LANES), F32), pltpu.VMEM((S5_W // 128, tl, 128), F32),
                        pltpu.VMEM((8, 2 * S5_LANES), F32)],
        compiler_params=pltpu.CompilerParams(
            dimension_semantics=("parallel", "arbitrary"), vmem_limit_bytes=VMEM_LIMIT),
        name="s5",
    )(u, bm, cm, pw, d, gw, gb, og)


def _ffn_kernel(x_ref, hm_ref, hs_ref, h5_ref, wom_ref, wos_ref, wo5_ref, g_ref,
                wg_ref, wv_ref, cw_ref, wd_ref, out_ref, h_s, gext_s, act_s, carry_s, *, tm):
    l = pl.program_id(1)

    @pl.when(l == 0)
    def _():
        carry_s[...] = jnp.zeros_like(carry_s)

    x1 = (x_ref[0] + _dot(hm_ref[0], wom_ref[...]) + _dot(hs_ref[0], wos_ref[...])
          + _dot(h5_ref[0], wo5_ref[...]))
    out_ref[0] = x1
    ms = jnp.mean(x1 * x1, axis=-1, keepdims=True)
    h_s[...] = (x1 * lax.rsqrt(ms + EPS) * g_ref[...]).astype(BF16)

    for c in range(D_FF // FFN_FC):
        cols = slice(c * FFN_FC, (c + 1) * FFN_FC)
        g = _dot(h_s[...], wg_ref[:, cols])
        v = _dot(h_s[...], wv_ref[:, cols])
        gext_s[0:8, :] = carry_s[c]
        gext_s[8:8 + tm, :] = g
        carry_s[c] = g[tm - 8:tm, :]
        conv = (cw_ref[2:3, cols] * g + cw_ref[1:2, cols] * gext_s[7:7 + tm, :]
                + cw_ref[0:1, cols] * gext_s[6:6 + tm, :])
        act_s[:, cols] = (conv * _sigmoid(conv) * v).astype(BF16)

    out_ref[0] += _dot(act_s[...], wd_ref[...])


def _ffn(x, hm, hs, h5, wom, wos, wo5, g, wg, wv, cw, wd):
    B, L, _ = x.shape
    tm = min(FFN_TM, L)
    grid = (B, L // tm)
    tspec = lambda w_: pl.BlockSpec((1, tm, w_), lambda b, i: (b, i, 0))
    return pl.pallas_call(
        functools.partial(_ffn_kernel, tm=tm),
        grid=grid,
        in_specs=[tspec(D_MODEL), tspec(ML_WP), tspec(SB_W), tspec(S5_W),
                  _const_spec((ML_WP, D_MODEL)), _const_spec((SB_W, D_MODEL)), _const_spec((S5_W, D_MODEL)),
                  _const_spec((1, D_MODEL)), _const_spec((D_MODEL, D_FF)), _const_spec((D_MODEL, D_FF)),
                  _const_spec((3, D_FF)), _const_spec((D_FF, D_MODEL))],
        out_specs=tspec(D_MODEL),
        out_shape=jax.ShapeDtypeStruct((B, L, D_MODEL), F32),
        scratch_shapes=[pltpu.VMEM((tm, D_MODEL), BF16), pltpu.VMEM((tm + 8, FFN_FC), F32),
                        pltpu.VMEM((tm, D_FF), BF16), pltpu.VMEM((D_FF // FFN_FC, 8, FFN_FC), F32)],
        compiler_params=pltpu.CompilerParams(
            dimension_semantics=("parallel", "arbitrary"), vmem_limit_bytes=VMEM_LIMIT),
        name="ffn",
    )(x, hm, hs, h5, wom, wos, wo5, g, wg, wv, cw, wd)


def _pad_heads(w):
    lead = w.shape[:-1]
    w = w.reshape(lead + (ML_HEADS, ML_DH))
    w = jnp.pad(w, [(0, 0)] * len(lead) + [(0, 0), (0, ML_DHP - ML_DH)])
    return w.reshape(lead + (ML_WP,))


def _pad_lanes(w, n):
    return jnp.pad(w, [(0, 0)] * (w.ndim - 1) + [(0, n - w.shape[-1])])


def _s5_params(a_re, a_im, log_dt, b_re, b_im, c_re, c_im, nseg):
    lr = jnp.minimum(a_re, -1e-4)
    li = a_im
    dt = jnp.exp(log_dt)[:, None]
    mag = jnp.exp(lr * dt)
    br = mag * jnp.cos(li * dt)
    bi = mag * jnp.sin(li * dt)
    nr, ni = br - 1.0, bi
    den = lr * lr + li * li
    fr = (nr * lr + ni * li) / den
    fi = (ni * lr - nr * li) / den
    bbr = fr[..., None] * b_re - fi[..., None] * b_im
    bbi = fr[..., None] * b_im + fi[..., None] * b_re
    eye = jnp.eye(S5_GROUPS, dtype=F32)
    bm_re = jnp.einsum("gk,gph->ghkp", eye, bbr).reshape(S5_W, S5_LANES)
    bm_im = jnp.einsum("gk,gph->ghkp", eye, bbi).reshape(S5_W, S5_LANES)
    bm = jnp.concatenate([bm_re, bm_im], axis=1).astype(BF16)
    cm_re = jnp.einsum("gk,ghp->gpkh", eye, c_re).reshape(S5_LANES, S5_W)
    cm_im = jnp.einsum("gk,ghp->gpkh", eye, c_im).reshape(S5_LANES, S5_W)
    cm = jnp.concatenate([cm_re, -cm_im], axis=0).astype(BF16)
    def cpow2(r, i, n):
        for _ in range(n):
            r, i = r * r - i * i, 2.0 * r * i
        return r, i

    lr_, li_ = br.reshape(-1), bi.reshape(-1)
    assert nseg & (nseg - 1) == 0
    ar, ai = cpow2(lr_, li_, nseg.bit_length() - 1)
    pr, pi_ = [ar], [ai]
    for _ in range(7):
        r, i = pr[-1], pi_[-1]
        pr.append(r * ar - i * ai)
        pi_.append(r * ai + i * ar)
    row = jnp.arange(8)[:, None]
    planes = [jnp.broadcast_to(lr_[None, :], (8, S5_LANES)), jnp.broadcast_to(li_[None, :], (8, S5_LANES))]
    for k in (1, 2, 4):
        keep = row >= k
        planes.append(jnp.where(keep, pr[k - 1][None, :], 0.0))
        planes.append(jnp.where(keep, pi_[k - 1][None, :], 0.0))
    planes.append(jnp.stack(pr, axis=0))
    planes.append(jnp.stack(pi_, axis=0))
    pw = jnp.stack(planes, axis=0).astype(F32)
    return bm, cm, pw


def kernel(x, norm_mix_g, w_in, ml_conv_w, ml_conv_b, ml_gate_b, ml_out_g, sb_q_g, sb_k_g, sb_out_g, s5_a_re, s5_a_im, s5_log_dt, s5_b_re, s5_b_im, s5_c_re, s5_c_im, s5_d, s5_glu_w, s5_glu_b, s5_out_g, w_out, norm_ffn_g, ffn_w_up, ffn_conv_w, ffn_w_down):
    depth = w_in.shape[0]
    seg = jnp.kron(jnp.eye(SB_HEADS, dtype=F32), jnp.ones((SB_DH, SB_DH), F32)).astype(BF16)
    for l in range(depth):
        w = w_in[l]
        o0 = 0
        parts = []
        for wd_ in (ML_W, ML_W, ML_W, ML_W, 2 * ML_HEADS, SB_W, SB_W, SB_W, S5_W):
            parts.append(w[:, o0:o0 + wd_])
            o0 += wd_
        wq, wk, wv_, wo_, wif, wsq, wsk, wsv, wu = parts
        wfi = jnp.concatenate([wif[:, ML_HEADS:], wif[:, :ML_HEADS]], axis=1)
        w_all = jnp.concatenate(
            [_pad_heads(wq), _pad_heads(wk), _pad_heads(wv_), _pad_heads(wo_), wsq, wsk, wsv, wu,
             _pad_lanes(wif, 128), _pad_lanes(wfi, 128)], axis=1).astype(BF16)
        qg = jnp.tile(sb_q_g[l], SB_HEADS)[None, :] * (SB_DH ** -0.5 * math.log2(math.e))
        kg = jnp.tile(sb_k_g[l], SB_HEADS)[None, :]
        (qk, mv, mo, sq, sk, sv, u, gr1, gr2, gc1, gc2) = _in_proj(
            x, norm_mix_g[l][None, :], w_all, seg, qg, kg)

        cw = jnp.concatenate([_pad_heads(ml_conv_w[l][:, :ML_W]), _pad_heads(ml_conv_w[l][:, ML_W:])], axis=1)
        cb = jnp.concatenate([_pad_heads(ml_conv_b[l][:ML_W]), _pad_heads(ml_conv_b[l][ML_W:])])[None, :]
        gb = ml_gate_b[l]
        gb2 = jnp.concatenate([gb[ML_HEADS:], gb[:ML_HEADS]])
        hm = _mlstm(qk, mv, mo, gr1, gr2, gc1, gc2, cw, cb, gb[:, None], gb2[:, None], gb[None, :],
                    gb2[None, :], _pad_heads(ml_out_g[l].reshape(-1))[None, :])

        hs = _sb_attn(sq, sk, sv, sb_out_g[l].reshape(1, SB_W))

        bm, cm, pw = _s5_params(s5_a_re[l], s5_a_im[l], s5_log_dt[l], s5_b_re[l], s5_b_im[l],
                                s5_c_re[l], s5_c_im[l], min(S5_TL, x.shape[1]) // 8)
        h5 = _s5(u, bm, cm, pw, s5_d[l].reshape(1, S5_W), s5_glu_w[l].astype(BF16),
                 s5_glu_b[l][None, :], s5_out_g[l][None, :])

        wo = w_out[l]
        wom = jnp.pad(wo[:ML_W].reshape(ML_HEADS, ML_DH, D_MODEL),
                      [(0, 0), (0, ML_DHP - ML_DH), (0, 0)]).reshape(ML_WP, D_MODEL).astype(BF16)
        wos = wo[ML_W:ML_W + SB_W].astype(BF16)
        wo5 = wo[ML_W + SB_W:].astype(BF16)
        x = _ffn(x, hm, hs, h5, wom, wos, wo5, norm_ffn_g[l][None, :],
                 ffn_w_up[l][:, :D_FF].astype(BF16), ffn_w_up[l][:, D_FF:].astype(BF16),
                 ffn_conv_w[l], ffn_w_down[l].astype(BF16))
    return x
```

```python
import functools
import math

import jax
import jax.numpy as jnp
from jax import lax
from jax.experimental import pallas as pl
from jax.experimental.pallas import tpu as pltpu

F32 = jnp.float32
BF16 = jnp.bfloat16

D_MODEL = 1024
ML_HEADS = 4
ML_DH = 96
ML_DHP = 128
ML_WP = ML_HEADS * ML_DHP
ML_W = ML_HEADS * ML_DH
ML_CONV = 4
SB_HEADS = 6
SB_DH = 64
SB_W = SB_HEADS * SB_DH
S5_W = 256
S5_GROUPS = 16
S5_CH = 16
S5_STATE = 64
S5_LANES = S5_GROUPS * S5_STATE
D_FF = 2816
EPS = 1e-6
NEG = -1e30

C_QK, C_V, C_O, C_SQ, C_SK, C_SV, C_U, C_G1, C_G2, C_END = (
    0, 1024, 1536, 2048, 2432, 2816, 3200, 3456, 3584, 3712)

VMEM_LIMIT = 56 * 1024 * 1024

IN_TM = 512
ML_TL = 512
ML_C = 128
SB_TQ = 256
SB_TK = 256
S5_TL = 512
FFN_TM = 512
FFN_FC = 256


def _const_spec(shape):
    nd = len(shape)
    return pl.BlockSpec(shape, lambda *_: (0,) * nd, pipeline_mode=pl.Buffered(1))


def _dot(a, b):
    return jnp.dot(a, b, preferred_element_type=F32)


def _dot_nt(a, b):
    return lax.dot_general(a, b, (((1,), (1,)), ((), ())), preferred_element_type=F32)


def _dot_tn(a, b):
    return lax.dot_general(a, b, (((0,), (0,)), ((), ())), preferred_element_type=F32)


def _split(x):
    hi = x.astype(BF16)
    lo = (x - hi.astype(F32)).astype(BF16)
    return hi, lo


def _log_sigmoid(x):
    return jnp.minimum(x, 0.0) - jnp.log(1.0 + jnp.exp(-jnp.abs(x)))


def _sigmoid(x):
    return 1.0 / (1.0 + jnp.exp(-x))


def _in_proj_kernel(x_ref, g_ref, w_ref, seg_ref, qg_ref, kg_ref,
                    qk_ref, v_ref, o_ref, sq_ref, sk_ref, sv_ref, u_ref,
                    gr1_ref, gr2_ref, gc1_ref, gc2_ref):
    x = x_ref[0]
    ms = jnp.mean(x * x, axis=-1, keepdims=True)
    h = (x * lax.rsqrt(ms + EPS) * g_ref[...]).astype(BF16)

    def proj(a, b):
        return _dot(h, w_ref[:, a:b])

    def qk_norm(z, gain):
        ss = _dot((z * z).astype(BF16), seg_ref[...])
        return (z * lax.rsqrt(ss * (1.0 / SB_DH) + EPS) * gain).astype(BF16)

    qk_ref[0] = proj(C_QK, C_V)
    v_ref[0] = proj(C_V, C_O).astype(BF16)
    o_ref[0] = proj(C_O, C_SQ).astype(BF16)
    sq_ref[0] = qk_norm(proj(C_SQ, C_SK), qg_ref[...])
    sk_ref[0] = qk_norm(proj(C_SK, C_SV), kg_ref[...])
    sv_ref[0] = proj(C_SV, C_U).astype(BF16)
    zu = proj(C_U, C_G1)
    nseg = zu.shape[0] // 8
    for r in range(8):
        for hf in range(S5_W // 128):
            u_ref[0, hf, pl.ds(r, nseg, stride=8), :] = zu[r * nseg:(r + 1) * nseg, hf * 128:(hf + 1) * 128]
    z1 = proj(C_G1, C_G2)
    z2 = proj(C_G2, C_END)
    gc1_ref[0] = z1[:, :8]
    gc2_ref[0] = z2[:, :8]
    gr1_ref[0] = z1.T[:8, :]
    gr2_ref[0] = z2.T[:8, :]


def _in_proj(x, g, w, seg, qg, kg):
    B, L, _ = x.shape
    tm = min(IN_TM, L)
    grid = (B, L // tm)
    tok = lambda w_, dt: jax.ShapeDtypeStruct((B, L, w_), dt)
    tspec = lambda w_: pl.BlockSpec((1, tm, w_), lambda b, i: (b, i, 0))
    rspec = pl.BlockSpec((1, 8, tm), lambda b, i: (b, 0, i))
    return pl.pallas_call(
        _in_proj_kernel,
        grid=grid,
        in_specs=[tspec(D_MODEL), _const_spec((1, D_MODEL)), _const_spec((D_MODEL, C_END)),
                  _const_spec((SB_W, SB_W)), _const_spec((1, SB_W)), _const_spec((1, SB_W))],
        out_specs=[tspec(2 * ML_WP), tspec(ML_WP), tspec(ML_WP), tspec(SB_W), tspec(SB_W), tspec(SB_W),
                   pl.BlockSpec((1, S5_W // 128, tm, 128), lambda b, i: (b, 0, i, 0)),
                   rspec, rspec, tspec(8), tspec(8)],
        out_shape=[tok(2 * ML_WP, F32), tok(ML_WP, BF16), tok(ML_WP, BF16), tok(SB_W, BF16),
                   tok(SB_W, BF16), tok(SB_W, BF16), jax.ShapeDtypeStruct((B, S5_W // 128, L, 128), F32),
                   jax.ShapeDtypeStruct((B, 8, L), F32), jax.ShapeDtypeStruct((B, 8, L), F32),
                   tok(8, F32), tok(8, F32)],
        compiler_params=pltpu.CompilerParams(
            dimension_semantics=("parallel", "parallel"), vmem_limit_bytes=VMEM_LIMIT),
        name="in_proj",
    )(x, g, w, seg, qg, kg)


def _mlstm_kernel(qk_ref, v_ref, o_ref, gr1_ref, gr2_ref, gc1_ref, gc2_ref,
                  cw_ref, cb_ref, gbr1_ref, gbr2_ref, gbc1_ref, gbc2_ref, og_ref,
                  out_ref, ext_s, st_s, m_s, *, tl):
    l = pl.program_id(1)

    @pl.when(l == 0)
    def _():
        ext_s[0:8, :] = jnp.zeros((8, 2 * ML_WP), F32)
        st_s[...] = jnp.zeros_like(st_s)
        m_s[...] = jnp.zeros_like(m_s)

    pre = qk_ref[0]
    ext_s[8:8 + tl, :] = pre
    conv = (cw_ref[3:4, :] * pre + cw_ref[2:3, :] * ext_s[7:7 + tl, :]
            + cw_ref[1:2, :] * ext_s[6:6 + tl, :] + cw_ref[0:1, :] * ext_s[5:5 + tl, :] + cb_ref[...])
    ext_s[0:8, :] = pre[tl - 8:tl, :]
    qk = conv * _sigmoid(conv)
    q_all = qk[:, :ML_WP].astype(BF16)
    k_all = qk[:, ML_WP:] * (ML_DH ** -0.5)

    gr1 = gr1_ref[0] + gbr1_ref[...]
    gr2 = gr2_ref[0] + gbr2_ref[...]
    gc1 = gc1_ref[0] + gbc1_ref[...]
    gc2 = gc2_ref[0] + gbc2_ref[...]

    ri = lax.broadcasted_iota(jnp.int32, (ML_C, ML_C), 0)
    ci = lax.broadcasted_iota(jnp.int32, (ML_C, ML_C), 1)
    tril = ci <= ri
    tri_low = jnp.where(tril, 1.0, 0.0).astype(BF16)
    tri_up = jnp.where(ri <= ci, 1.0, 0.0).astype(BF16)
    lane = lax.broadcasted_iota(jnp.int32, (ML_C, ML_DHP), 1)

    for c in range(tl // ML_C):
        rows = slice(c * ML_C, (c + 1) * ML_C)
        ic = gc1[rows, :]
        lf_hi, lf_lo = _split(_log_sigmoid(gc2[rows, :]))
        b_col = _dot(tri_low, lf_hi) + _dot(tri_low, lf_lo)
        ir = gr1[:, rows]
        lr_hi, lr_lo = _split(_log_sigmoid(gr2[:, rows]))
        b_row = _dot(lr_hi, tri_up) + _dot(lr_lo, tri_up)
        r_row = ir - b_row
        btot = b_col[ML_C - 1:ML_C, :]
        mp = m_s[...]
        wlog = btot - b_col + ic
        m_loc = jnp.max(wlog, axis=0, keepdims=True)
        wexp = jnp.exp(wlog - m_loc)
        m_new = jnp.maximum(btot + mp, m_loc)
        a_sc = jnp.exp(btot + mp - m_new)
        c_sc = jnp.exp(m_loc - m_new)
        inter_log = b_col + mp

        for hd in range(ML_HEADS):
            cols = slice(hd * ML_DHP, (hd + 1) * ML_DHP)
            dlog = jnp.where(tril, b_col[:, hd:hd + 1] + r_row[hd:hd + 1, :], NEG)
            m_t = jnp.maximum(inter_log[:, hd:hd + 1], jnp.max(dlog, axis=-1, keepdims=True))
            dmat = jnp.exp(dlog - m_t)
            qh = q_all[rows, cols]
            kh = k_all[rows, cols]
            s = _dot_nt(qh, kh.astype(BF16))
            p = (s * dmat).astype(BF16)
            v_aug = jnp.where(lane == ML_DH, 1.0, v_ref[0, rows, cols].astype(F32)).astype(BF16)
            st = st_s[hd]
            isc = jnp.exp(inter_log[:, hd:hd + 1] - m_t)
            acc = _dot(p, v_aug) + isc * _dot(qh, st.astype(BF16))
            den = acc[:, ML_DH:ML_DH + 1]
            hv = acc * (1.0 / jnp.maximum(jnp.abs(den), jnp.exp(-m_t)))
            hv = jnp.where(lane < ML_DH, hv, 0.0)
            ss = jnp.sum(hv * hv, axis=-1, keepdims=True) * (1.0 / ML_DH)
            hn = hv * lax.rsqrt(ss + EPS) * og_ref[:, cols] * _sigmoid(o_ref[0, rows, cols].astype(F32))
            out_ref[0, rows, cols] = hn.astype(BF16)
            kw = (kh * wexp[:, hd:hd + 1]).astype(BF16)
            st_s[hd] = a_sc[:, hd:hd + 1] * st + c_sc[:, hd:hd + 1] * _dot_tn(kw, v_aug)
        m_s[...] = m_new


def _mlstm(qk, v, o, gr1, gr2, gc1, gc2, cw, cb, gbr1, gbr2, gbc1, gbc2, og):
    B, L, _ = qk.shape
    tl = min(ML_TL, L)
    grid = (B, L // tl)
    tspec = lambda w_: pl.BlockSpec((1, tl, w_), lambda b, i: (b, i, 0))
    rspec = pl.BlockSpec((1, 8, tl), lambda b, i: (b, 0, i))
    return pl.pallas_call(
        functools.partial(_mlstm_kernel, tl=tl),
        grid=grid,
        in_specs=[tspec(2 * ML_WP), tspec(ML_WP), tspec(ML_WP), rspec, rspec, tspec(8), tspec(8),
                  _const_spec((ML_CONV, 2 * ML_WP)), _const_spec((1, 2 * ML_WP)),
                  _const_spec((8, 1)), _const_spec((8, 1)), _const_spec((1, 8)), _const_spec((1, 8)),
                  _const_spec((1, ML_WP))],
        out_specs=tspec(ML_WP),
        out_shape=jax.ShapeDtypeStruct((B, L, ML_WP), BF16),
        scratch_shapes=[pltpu.VMEM((tl + 8, 2 * ML_WP), F32),
                        pltpu.VMEM((ML_HEADS, ML_DHP, ML_DHP), F32),
                        pltpu.VMEM((1, 8), F32)],
        compiler_params=pltpu.CompilerParams(
            dimension_semantics=("parallel", "arbitrary"), vmem_limit_bytes=VMEM_LIMIT),
        name="mlstm",
    )(qk, v, o, gr1, gr2, gc1, gc2, cw, cb, gbr1, gbr2, gbc1, gbc2, og)


def _sb_kernel(stop_ref, q_ref, k_ref, v_ref, og_ref, out_ref, acc_s, r_s):
    i = pl.program_id(1)
    npair = SB_HEADS // 2
    lane = lax.broadcasted_iota(jnp.int32, (SB_TQ, 2 * SB_DH), 1)
    first = lane < SB_DH
    q_heads = []
    for p in range(npair):
        q = q_ref[0, :, p * 128:(p + 1) * 128]
        zero = jnp.zeros_like(q)
        q_heads.append(jnp.where(first, q, zero))
        q_heads.append(jnp.where(first, zero, q))
    ri = lax.broadcasted_iota(jnp.int32, (SB_TQ, SB_TK), 0)
    ci = lax.broadcasted_iota(jnp.int32, (SB_TQ, SB_TK), 1)
    rj = lax.broadcasted_iota(jnp.int32, (SB_TK, SB_TK), 0)
    cj = lax.broadcasted_iota(jnp.int32, (SB_TK, SB_TK), 1)
    suffix = jnp.where(rj >= cj, 1.0, 0.0).astype(BF16)
    sign = jnp.uint32(0x80000000)

    acc_s[...] = jnp.zeros_like(acc_s)
    r_s[...] = jnp.zeros_like(r_s)

    def blocks(tiles):
        units = [(t, hd) for t in range(len(tiles)) for hd in range(SB_HEADS)]
        ks, vs, zs, sbs = {}, {}, {}, {}
        for t, (j, _) in enumerate(tiles):
            start = pl.multiple_of(j * SB_TK, SB_TK)
            for p in range(npair):
                ks[t, p] = k_ref[0, pl.ds(start, SB_TK), p * 128:(p + 1) * 128]
                vs[t, p] = v_ref[0, pl.ds(start, SB_TK), p * 128:(p + 1) * 128]

        def scores(u):
            t, hd = u
            zs[u] = _dot_nt(q_heads[hd], ks[t, hd // 2])

        def suffix_sums(u):
            causal = tiles[u[0]][1]
            z = zs[u]
            neg_abs = lax.bitcast_convert_type(lax.bitcast_convert_type(z, jnp.uint32) | sign, F32)
            sp = jnp.maximum(z, 0.0) + jnp.log2(1.0 + jnp.exp2(neg_abs))
            if causal is not None:
                sp = jnp.where(causal, sp, 0.0)
            sbs[u] = _dot(sp.astype(BF16), suffix)

        def weights_and_values(u):
            t, hd = u
            causal = tiles[t][1]
            r = r_s[hd]
            a = jnp.exp2(zs[u] - (sbs[u] + r))
            if causal is not None:
                a = jnp.where(causal, a, 0.0)
            r_s[hd] = r + sbs[u][:, 0:1]
            acc_s[hd] += _dot(a.astype(BF16), vs[t, hd // 2])

        stages = (scores, suffix_sums, weights_and_values)
        for step in range(len(units) + len(stages) - 1):
            for lag, stage in enumerate(stages):
                if 0 <= step - lag < len(units):
                    stage(units[step - lag])

    nd = SB_TQ // SB_TK
    blocks([(i * nd + d, ci + d * SB_TK < ri) for d in reversed(range(nd))])

    def min_suffix():
        m = r_s[0]
        for hd in range(1, SB_HEADS):
            m = jnp.minimum(m, r_s[hd])
        return jnp.min(m)

    n_left = i * nd

    def cond(carry):
        t, rmin = carry
        return jnp.logical_and(t < n_left, rmin <= stop_ref[0])

    def body(carry):
        t, _ = carry
        blocks([(n_left - 1 - t, None)])
        return t + 1, min_suffix()

    lax.while_loop(cond, body, (jnp.int32(0), min_suffix()))

    for p in range(npair):
        o = jnp.where(first, acc_s[2 * p], acc_s[2 * p + 1])
        o2 = o * o
        ss_a = jnp.sum(jnp.where(first, o2, 0.0), axis=-1, keepdims=True)
        ss_b = jnp.sum(jnp.where(first, 0.0, o2), axis=-1, keepdims=True)
        ss = jnp.where(first, ss_a, ss_b) * (1.0 / SB_DH)
        out_ref[0, :, p * 128:(p + 1) * 128] = (
            o * lax.rsqrt(ss + EPS) * og_ref[:, p * 128:(p + 1) * 128]).astype(BF16)


def _sb_attn(stop, q, k, v, og):
    B, L, _ = q.shape
    assert L % SB_TQ == 0 and SB_TQ % SB_TK == 0
    grid = (B, L // SB_TQ)
    return pl.pallas_call(
        _sb_kernel,
        grid=grid,
        in_specs=[pl.BlockSpec(memory_space=pltpu.SMEM),
                  pl.BlockSpec((1, SB_TQ, SB_W), lambda b, i: (b, i, 0)),
                  pl.BlockSpec((1, L, SB_W), lambda b, i: (b, 0, 0)),
                  pl.BlockSpec((1, L, SB_W), lambda b, i: (b, 0, 0)),
                  _const_spec((1, SB_W))],
        out_specs=pl.BlockSpec((1, SB_TQ, SB_W), lambda b, i: (b, i, 0)),
        out_shape=jax.ShapeDtypeStruct((B, L, SB_W), BF16),
        scratch_shapes=[pltpu.VMEM((SB_HEADS, SB_TQ, 2 * SB_DH), F32),
                        pltpu.VMEM((SB_HEADS, SB_TQ, 1), F32)],
        compiler_params=pltpu.CompilerParams(
            dimension_semantics=("parallel", "parallel"), vmem_limit_bytes=VMEM_LIMIT),
        name="sb_attn",
    )(stop, q, k, v, og)


def _gelu_tanh(x):
    return 0.5 * x * (1.0 + jnp.tanh(math.sqrt(2.0 / math.pi) * (x + 0.044715 * (x * x * x))))


def _cmul(ar, ai, br, bi):
    return ar * br - ai * bi, ar * bi + ai * br


def _s5_kernel(u_ref, bm_ref, cm_ref, pw_ref, d_ref, gw_ref, gb_ref, og_ref, out_ref, xs_s, y_s, carry_s, *, tl):
    l = pl.program_id(1)
    nseg = tl // 8
    nlt = S5_LANES // 128
    grp = 8

    @pl.when(l == 0)
    def _():
        carry_s[...] = jnp.zeros_like(carry_s)

    u = jnp.concatenate([u_ref[0, hf] for hf in range(S5_W // 128)], axis=-1)
    xs_s[...] = _dot(u.astype(BF16), bm_ref[...])

    def lanes(lt):
        return slice(lt * 128, (lt + 1) * 128), slice(S5_LANES + lt * 128, S5_LANES + (lt + 1) * 128)

    for g0 in range(0, nlt, grp):
        tiles = range(g0, g0 + grp)
        lam = [(pw_ref[0, :, lanes(lt)[0]], pw_ref[1, :, lanes(lt)[0]]) for lt in tiles]

        def local_step(j, st):
            r0 = pl.multiple_of(j * 8, 8)
            new = []
            for n, lt in enumerate(tiles):
                re_l, im_l = lanes(lt)
                pr, pi_ = _cmul(lam[n][0], lam[n][1], st[2 * n], st[2 * n + 1])
                xr = xs_s[pl.ds(r0, 8), re_l] + pr
                xi = xs_s[pl.ds(r0, 8), im_l] + pi_
                xs_s[pl.ds(r0, 8), re_l] = xr
                xs_s[pl.ds(r0, 8), im_l] = xi
                new += [xr, xi]
            return tuple(new)

        fin = lax.fori_loop(0, nseg, local_step, tuple(jnp.zeros((8, 128), F32) for _ in range(2 * grp)))

        start = []
        for n, lt in enumerate(tiles):
            re_l, im_l = lanes(lt)
            er, ei = fin[2 * n], fin[2 * n + 1]
            for m, k in enumerate((1, 2, 4)):
                pr, pi_ = _cmul(pw_ref[2 + 2 * m, :, re_l], pw_ref[3 + 2 * m, :, re_l],
                                pltpu.roll(er, k, 0), pltpu.roll(ei, k, 0))
                er, ei = er + pr, ei + pi_
            s0r, s0i = carry_s[:, re_l], carry_s[:, im_l]
            pr, pi_ = _cmul(pw_ref[8, :, re_l], pw_ref[9, :, re_l], s0r, s0i)
            er, ei = er + pr, ei + pi_
            carry_s[:, re_l] = jnp.broadcast_to(er[7:8, :], (8, 128))
            carry_s[:, im_l] = jnp.broadcast_to(ei[7:8, :], (8, 128))
            row = lax.broadcasted_iota(jnp.int32, (8, 128), 0)
            start += [jnp.where(row == 0, s0r, pltpu.roll(er, 1, 0)),
                      jnp.where(row == 0, s0i, pltpu.roll(ei, 1, 0))]

        def fix_step(j, tr):
            r0 = pl.multiple_of(j * 8, 8)
            new = []
            for n, lt in enumerate(tiles):
                re_l, im_l = lanes(lt)
                pr, pi_ = _cmul(lam[n][0], lam[n][1], tr[2 * n], tr[2 * n + 1])
                xs_s[pl.ds(r0, 8), re_l] += pr
                xs_s[pl.ds(r0, 8), im_l] += pi_
                new += [pr, pi_]
            return tuple(new)

        lax.fori_loop(0, nseg, fix_step, tuple(start))

    y = _dot(xs_s[...].astype(BF16), cm_ref[...]) + d_ref[...] * u
    y = _gelu_tanh(y)
    y = y * _sigmoid(_dot(y.astype(BF16), gw_ref[...]) + gb_ref[...])
    ms = jnp.mean(y * y, axis=-1, keepdims=True)
    y = y * lax.rsqrt(ms + EPS) * og_ref[...]
    for hf in range(S5_W // 128):
        y_s[hf] = y[:, hf * 128:(hf + 1) * 128]
    for r in range(8):
        out_ref[0, r * nseg:(r + 1) * nseg, :] = jnp.concatenate(
            [y_s[hf, pl.ds(r, nseg, stride=8), :] for hf in range(S5_W // 128)], axis=-1).astype(BF16)


def _s5(u, bm, cm, pw, d, gw, gb, og):
    B, _, L, _ = u.shape
    tl = min(S5_TL, L)
    grid = (B, L // tl)
    tspec = pl.BlockSpec((1, tl, S5_W), lambda b, i: (b, i, 0))
    uspec = pl.BlockSpec((1, S5_W // 128, tl, 128), lambda b, i: (b, 0, i, 0))
    return pl.pallas_call(
        functools.partial(_s5_kernel, tl=tl),
        grid=grid,
        in_specs=[uspec, _const_spec((S5_W, 2 * S5_LANES)), _const_spec((2 * S5_LANES, S5_W)),
                  _const_spec((10, 8, S5_LANES)), _const_spec((1, S5_W)), _const_spec((S5_W, S5_W)),
                  _const_spec((1, S5_W)), _const_spec((1, S5_W))],
        out_specs=tspec,
        out_shape=jax.ShapeDtypeStruct((B, L, S5_W), BF16),
        scratch_shapes=[pltpu.VMEM((tl, 2 * S5_LANES), F32), pltpu.VMEM((S5_W // 128, tl, 128), F32),
                        pltpu.VMEM((8, 2 * S5_LANES), F32)],
        compiler_params=pltpu.CompilerParams(
            dimension_semantics=("parallel", "arbitrary"), vmem_limit_bytes=VMEM_LIMIT),
        name="s5",
    )(u, bm, cm, pw, d, gw, gb, og)


def _ffn_kernel(x_ref, hm_ref, hs_ref, h5_ref, wom_ref, wos_ref, wo5_ref, g_ref,
                wg_ref, wv_ref, cw_ref, wd_ref, out_ref, h_s, gext_s, act_s, carry_s, *, tm):
    l = pl.program_id(1)

    @pl.when(l == 0)
    def _():
        carry_s[...] = jnp.zeros_like(carry_s)

    x1 = (x_ref[0] + _dot(hm_ref[0], wom_ref[...]) + _dot(hs_ref[0], wos_ref[...])
          + _dot(h5_ref[0], wo5_ref[...]))
    out_ref[0] = x1
    ms = jnp.mean(x1 * x1, axis=-1, keepdims=True)
    h_s[...] = (x1 * lax.rsqrt(ms + EPS) * g_ref[...]).astype(BF16)

    for c in range(D_FF // FFN_FC):
        cols = slice(c * FFN_FC, (c + 1) * FFN_FC)
        g = _dot(h_s[...], wg_ref[:, cols])
        v = _dot(h_s[...], wv_ref[:, cols])
        gext_s[0:8, :] = carry_s[c]
        gext_s[8:8 + tm, :] = g
        carry_s[c] = g[tm - 8:tm, :]
        conv = (cw_ref[2:3, cols] * g + cw_ref[1:2, cols] * gext_s[7:7 + tm, :]
                + cw_ref[0:1, cols] * gext_s[6:6 + tm, :])
        act_s[:, cols] = (conv * _sigmoid(conv) * v).astype(BF16)

    out_ref[0] += _dot(act_s[...], wd_ref[...])


def _ffn(x, hm, hs, h5, wom, wos, wo5, g, wg, wv, cw, wd):
    B, L, _ = x.shape
    tm = min(FFN_TM, L)
    grid = (B, L // tm)
    tspec = lambda w_: pl.BlockSpec((1, tm, w_), lambda b, i: (b, i, 0))
    return pl.pallas_call(
        functools.partial(_ffn_kernel, tm=tm),
        grid=grid,
        in_specs=[tspec(D_MODEL), tspec(ML_WP), tspec(SB_W), tspec(S5_W),
                  _const_spec((ML_WP, D_MODEL)), _const_spec((SB_W, D_MODEL)), _const_spec((S5_W, D_MODEL)),
                  _const_spec((1, D_MODEL)), _const_spec((D_MODEL, D_FF)), _const_spec((D_MODEL, D_FF)),
                  _const_spec((3, D_FF)), _const_spec((D_FF, D_MODEL))],
        out_specs=tspec(D_MODEL),
        out_shape=jax.ShapeDtypeStruct((B, L, D_MODEL), F32),
        scratch_shapes=[pltpu.VMEM((tm, D_MODEL), BF16), pltpu.VMEM((tm + 8, FFN_FC), F32),
                        pltpu.VMEM((tm, D_FF), BF16), pltpu.VMEM((D_FF // FFN_FC, 8, FFN_FC), F32)],
        compiler_params=pltpu.CompilerParams(
            dimension_semantics=("parallel", "arbitrary"), vmem_limit_bytes=VMEM_LIMIT),
        name="ffn",
    )(x, hm, hs, h5, wom, wos, wo5, g, wg, wv, cw, wd)


def _pad_heads(w):
    lead = w.shape[:-1]
    w = w.reshape(lead + (ML_HEADS, ML_DH))
    w = jnp.pad(w, [(0, 0)] * len(lead) + [(0, 0), (0, ML_DHP - ML_DH)])
    return w.reshape(lead + (ML_WP,))


def _pad_lanes(w, n):
    return jnp.pad(w, [(0, 0)] * (w.ndim - 1) + [(0, n - w.shape[-1])])


def _s5_params(a_re, a_im, log_dt, b_re, b_im, c_re, c_im, nseg):
    lr = jnp.minimum(a_re, -1e-4)
    li = a_im
    dt = jnp.exp(log_dt)[:, None]
    mag = jnp.exp(lr * dt)
    br = mag * jnp.cos(li * dt)
    bi = mag * jnp.sin(li * dt)
    nr, ni = br - 1.0, bi
    den = lr * lr + li * li
    fr = (nr * lr + ni * li) / den
    fi = (ni * lr - nr * li) / den
    bbr = fr[..., None] * b_re - fi[..., None] * b_im
    bbi = fr[..., None] * b_im + fi[..., None] * b_re
    eye = jnp.eye(S5_GROUPS, dtype=F32)
    bm_re = jnp.einsum("gk,gph->ghkp", eye, bbr).reshape(S5_W, S5_LANES)
    bm_im = jnp.einsum("gk,gph->ghkp", eye, bbi).reshape(S5_W, S5_LANES)
    bm = jnp.concatenate([bm_re, bm_im], axis=1).astype(BF16)
    cm_re = jnp.einsum("gk,ghp->gpkh", eye, c_re).reshape(S5_LANES, S5_W)
    cm_im = jnp.einsum("gk,ghp->gpkh", eye, c_im).reshape(S5_LANES, S5_W)
    cm = jnp.concatenate([cm_re, -cm_im], axis=0).astype(BF16)
    def cpow2(r, i, n):
        for _ in range(n):
            r, i = r * r - i * i, 2.0 * r * i
        return r, i

    lr_, li_ = br.reshape(-1), bi.reshape(-1)
    assert nseg & (nseg - 1) == 0
    ar, ai = cpow2(lr_, li_, nseg.bit_length() - 1)
    pr, pi_ = [ar], [ai]
    for _ in range(7):
        r, i = pr[-1], pi_[-1]
        pr.append(r * ar - i * ai)
        pi_.append(r * ai + i * ar)
    row = jnp.arange(8)[:, None]
    planes = [jnp.broadcast_to(lr_[None, :], (8, S5_LANES)), jnp.broadcast_to(li_[None, :], (8, S5_LANES))]
    for k in (1, 2, 4):
        keep = row >= k
        planes.append(jnp.where(keep, pr[k - 1][None, :], 0.0))
        planes.append(jnp.where(keep, pi_[k - 1][None, :], 0.0))
    planes.append(jnp.stack(pr, axis=0))
    planes.append(jnp.stack(pi_, axis=0))
    pw = jnp.stack(planes, axis=0).astype(F32)
    return bm, cm, pw


def kernel(x, norm_mix_g, w_in, ml_conv_w, ml_conv_b, ml_gate_b, ml_out_g, sb_q_g, sb_k_g, sb_out_g, s5_a_re, s5_a_im, s5_log_dt, s5_b_re, s5_b_im, s5_c_re, s5_c_im, s5_d, s5_glu_w, s5_glu_b, s5_out_g, w_out, norm_ffn_g, ffn_w_up, ffn_conv_w, ffn_w_down):
    depth = w_in.shape[0]
    seg = jnp.kron(jnp.eye(SB_HEADS, dtype=F32), jnp.ones((SB_DH, SB_DH), F32)).astype(BF16)
    for l in range(depth):
        w = w_in[l]
        o0 = 0
        parts = []
        for wd_ in (ML_W, ML_W, ML_W, ML_W, 2 * ML_HEADS, SB_W, SB_W, SB_W, S5_W):
            parts.append(w[:, o0:o0 + wd_])
            o0 += wd_
        wq, wk, wv_, wo_, wif, wsq, wsk, wsv, wu = parts
        wfi = jnp.concatenate([wif[:, ML_HEADS:], wif[:, :ML_HEADS]], axis=1)
        w_all = jnp.concatenate(
            [_pad_heads(wq), _pad_heads(wk), _pad_heads(wv_), _pad_heads(wo_), wsq, wsk, wsv, wu,
             _pad_lanes(wif, 128), _pad_lanes(wfi, 128)], axis=1).astype(BF16)
        qg = jnp.tile(sb_q_g[l], SB_HEADS)[None, :] * (SB_DH ** -0.5 * math.log2(math.e))
        kg = jnp.tile(sb_k_g[l], SB_HEADS)[None, :]
        (qk, mv, mo, sq, sk, sv, u, gr1, gr2, gc1, gc2) = _in_proj(
            x, norm_mix_g[l][None, :], w_all, seg, qg, kg)

        cw = jnp.concatenate([_pad_heads(ml_conv_w[l][:, :ML_W]), _pad_heads(ml_conv_w[l][:, ML_W:])], axis=1)
        cb = jnp.concatenate([_pad_heads(ml_conv_b[l][:ML_W]), _pad_heads(ml_conv_b[l][ML_W:])])[None, :]
        gb = ml_gate_b[l]
        gb2 = jnp.concatenate([gb[ML_HEADS:], gb[:ML_HEADS]])
        hm = _mlstm(qk, mv, mo, gr1, gr2, gc1, gc2, cw, cb, gb[:, None], gb2[:, None], gb[None, :],
                    gb2[None, :], _pad_heads(ml_out_g[l].reshape(-1))[None, :])

        stop = (1.05 * SB_DH * jnp.max(jnp.abs(qg)) * jnp.max(jnp.abs(kg)) + 160.0).reshape(1).astype(F32)
        hs = _sb_attn(stop, sq, sk, sv, sb_out_g[l].reshape(1, SB_W))

        bm, cm, pw = _s5_params(s5_a_re[l], s5_a_im[l], s5_log_dt[l], s5_b_re[l], s5_b_im[l],
                                s5_c_re[l], s5_c_im[l], min(S5_TL, x.shape[1]) // 8)
        h5 = _s5(u, bm, cm, pw, s5_d[l].reshape(1, S5_W), s5_glu_w[l].astype(BF16),
                 s5_glu_b[l][None, :], s5_out_g[l][None, :])

        wo = w_out[l]
        wom = jnp.pad(wo[:ML_W].reshape(ML_HEADS, ML_DH, D_MODEL),
                      [(0, 0), (0, ML_DHP - ML_DH), (0, 0)]).reshape(ML_WP, D_MODEL).astype(BF16)
        wos = wo[ML_W:ML_W + SB_W].astype(BF16)
        wo5 = wo[ML_W + SB_W:].astype(BF16)
        x = _ffn(x, hm, hs, h5, wom, wos, wo5, norm_ffn_g[l][None, :],
                 ffn_w_up[l][:, :D_FF].astype(BF16), ffn_w_up[l][:, D_FF:].astype(BF16),
                 ffn_conv_w[l], ffn_w_down[l].astype(BF16))
    return x
```

```python
import functools
import math

import jax
import jax.numpy as jnp
from jax import lax
from jax.experimental import pallas as pl
from jax.experimental.pallas import tpu as pltpu

F32 = jnp.float32
BF16 = jnp.bfloat16

D_MODEL = 1024
ML_HEADS = 4
ML_DH = 96
ML_DHP = 128
ML_WP = ML_HEADS * ML_DHP
ML_W = ML_HEADS * ML_DH
ML_CONV = 4
SB_HEADS = 6
SB_DH = 64
SB_W = SB_HEADS * SB_DH
S5_W = 256
S5_GROUPS = 16
S5_CH = 16
S5_STATE = 64
S5_LANES = S5_GROUPS * S5_STATE
D_FF = 2816
EPS = 1e-6
NEG = -1e30

C_QK, C_V, C_O, C_SQ, C_SK, C_SV, C_U, C_G1, C_G2, C_END = (
    0, 1024, 1536, 2048, 2432, 2816, 3200, 3456, 3584, 3712)

VMEM_LIMIT = 56 * 1024 * 1024

IN_TM = 1024
ML_TL = 512
ML_C = 128
SB_TQ = 256
SB_TK = 256
S5_TL = 1024
FFN_TM = 512
FFN_FC = 256


def _const_spec(shape):
    nd = len(shape)
    return pl.BlockSpec(shape, lambda *_: (0,) * nd, pipeline_mode=pl.Buffered(1))


def _dot(a, b):
    return jnp.dot(a, b, preferred_element_type=F32)


def _dot_nt(a, b):
    return lax.dot_general(a, b, (((1,), (1,)), ((), ())), preferred_element_type=F32)


def _dot_tn(a, b):
    return lax.dot_general(a, b, (((0,), (0,)), ((), ())), preferred_element_type=F32)


def _split(x):
    hi = x.astype(BF16)
    lo = (x - hi.astype(F32)).astype(BF16)
    return hi, lo


def _log_sigmoid(x):
    return jnp.minimum(x, 0.0) - jnp.log(1.0 + jnp.exp(-jnp.abs(x)))


def _sigmoid(x):
    return 1.0 / (1.0 + jnp.exp(-x))


def _in_proj_kernel(x_ref, g_ref, w_ref, seg_ref, qg_ref, kg_ref,
                    qk_ref, v_ref, o_ref, sq_ref, sk_ref, sv_ref, u_ref,
                    gr1_ref, gr2_ref, gc1_ref, gc2_ref):
    x = x_ref[0]
    ms = jnp.mean(x * x, axis=-1, keepdims=True)
    h = (x * lax.rsqrt(ms + EPS) * g_ref[...]).astype(BF16)

    def proj(a, b):
        return _dot(h, w_ref[:, a:b])

    def qk_norm(z, gain):
        ss = _dot((z * z).astype(BF16), seg_ref[...])
        return (z * lax.rsqrt(ss * (1.0 / SB_DH) + EPS) * gain).astype(BF16)

    qk_ref[0] = proj(C_QK, C_V)
    v_ref[0] = proj(C_V, C_O).astype(BF16)
    o_ref[0] = proj(C_O, C_SQ).astype(BF16)
    sq_ref[0] = qk_norm(proj(C_SQ, C_SK), qg_ref[...])
    sk_ref[0] = qk_norm(proj(C_SK, C_SV), kg_ref[...])
    sv_ref[0] = proj(C_SV, C_U).astype(BF16)
    zu = proj(C_U, C_G1)
    nseg = zu.shape[0] // 8
    for r in range(8):
        for hf in range(S5_W // 128):
            u_ref[0, hf, pl.ds(r, nseg, stride=8), :] = zu[r * nseg:(r + 1) * nseg, hf * 128:(hf + 1) * 128]
    z1 = proj(C_G1, C_G2)
    z2 = proj(C_G2, C_END)
    gc1_ref[0] = z1[:, :8]
    gc2_ref[0] = z2[:, :8]
    gr1_ref[0] = z1.T[:8, :]
    gr2_ref[0] = z2.T[:8, :]


def _in_proj(x, g, w, seg, qg, kg):
    B, L, _ = x.shape
    tm = min(IN_TM, L)
    grid = (B, L // tm)
    tok = lambda w_, dt: jax.ShapeDtypeStruct((B, L, w_), dt)
    tspec = lambda w_: pl.BlockSpec((1, tm, w_), lambda b, i: (b, i, 0))
    rspec = pl.BlockSpec((1, 8, tm), lambda b, i: (b, 0, i))
    return pl.pallas_call(
        _in_proj_kernel,
        grid=grid,
        in_specs=[tspec(D_MODEL), _const_spec((1, D_MODEL)), _const_spec((D_MODEL, C_END)),
                  _const_spec((SB_W, SB_W)), _const_spec((1, SB_W)), _const_spec((1, SB_W))],
        out_specs=[tspec(2 * ML_WP), tspec(ML_WP), tspec(ML_WP), tspec(SB_W), tspec(SB_W), tspec(SB_W),
                   pl.BlockSpec((1, S5_W // 128, tm, 128), lambda b, i: (b, 0, i, 0)),
                   rspec, rspec, tspec(8), tspec(8)],
        out_shape=[tok(2 * ML_WP, F32), tok(ML_WP, BF16), tok(ML_WP, BF16), tok(SB_W, BF16),
                   tok(SB_W, BF16), tok(SB_W, BF16), jax.ShapeDtypeStruct((B, S5_W // 128, L, 128), F32),
                   jax.ShapeDtypeStruct((B, 8, L), F32), jax.ShapeDtypeStruct((B, 8, L), F32),
                   tok(8, F32), tok(8, F32)],
        compiler_params=pltpu.CompilerParams(
            dimension_semantics=("parallel", "parallel"), vmem_limit_bytes=VMEM_LIMIT),
        name="in_proj",
    )(x, g, w, seg, qg, kg)


def _mlstm_kernel(qk_ref, v_ref, o_ref, gr1_ref, gr2_ref, gc1_ref, gc2_ref,
                  cw_ref, cb_ref, gbr1_ref, gbr2_ref, gbc1_ref, gbc2_ref, og_ref,
                  out_ref, ext_s, st_s, m_s, *, tl):
    l = pl.program_id(1)

    @pl.when(l == 0)
    def _():
        ext_s[0:8, :] = jnp.zeros((8, 2 * ML_WP), F32)
        st_s[...] = jnp.zeros_like(st_s)
        m_s[...] = jnp.zeros_like(m_s)

    pre = qk_ref[0]
    ext_s[8:8 + tl, :] = pre
    conv = (cw_ref[3:4, :] * pre + cw_ref[2:3, :] * ext_s[7:7 + tl, :]
            + cw_ref[1:2, :] * ext_s[6:6 + tl, :] + cw_ref[0:1, :] * ext_s[5:5 + tl, :] + cb_ref[...])
    ext_s[0:8, :] = pre[tl - 8:tl, :]
    qk = conv * _sigmoid(conv)
    q_all = qk[:, :ML_WP].astype(BF16)
    k_all = qk[:, ML_WP:] * (ML_DH ** -0.5)

    gr1 = gr1_ref[0] + gbr1_ref[...]
    gr2 = gr2_ref[0] + gbr2_ref[...]
    gc1 = gc1_ref[0] + gbc1_ref[...]
    gc2 = gc2_ref[0] + gbc2_ref[...]

    ri = lax.broadcasted_iota(jnp.int32, (ML_C, ML_C), 0)
    ci = lax.broadcasted_iota(jnp.int32, (ML_C, ML_C), 1)
    visible = ri <= ci
    tri_low = jnp.where(ci <= ri, 1.0, 0.0).astype(BF16)
    tri_up = jnp.where(visible, 1.0, 0.0).astype(BF16)
    lane = lax.broadcasted_iota(jnp.int32, (ML_C, ML_DHP), 1)

    for c in range(tl // ML_C):
        rows = slice(c * ML_C, (c + 1) * ML_C)
        lf_hi, lf_lo = _split(_log_sigmoid(gc2[rows, :]))
        r_col = gc1[rows, :] - (_dot(tri_low, lf_hi) + _dot(tri_low, lf_lo))
        ir = gr1[:, rows]
        lr_hi, lr_lo = _split(_log_sigmoid(gr2[:, rows]))
        b_row = _dot(lr_hi, tri_up) + _dot(lr_lo, tri_up)
        btot = b_row[:, ML_C - 1:ML_C]
        mp = m_s[...]
        wlog = btot - b_row + ir
        m_loc = jnp.max(wlog, axis=-1, keepdims=True)
        wexp = jnp.exp(wlog - m_loc)
        m_new = jnp.maximum(btot + mp, m_loc)
        a_sc = jnp.exp(btot + mp - m_new)
        c_sc = jnp.exp(m_loc - m_new)
        inter_log = b_row + mp

        for hd in range(ML_HEADS):
            cols = slice(hd * ML_DHP, (hd + 1) * ML_DHP)
            head = slice(hd, hd + 1)
            qh = q_all[rows, cols]
            kh = k_all[rows, cols].astype(BF16)
            dlog = jnp.where(visible, r_col[:, head] + b_row[head, :], NEG)
            m_t = jnp.maximum(inter_log[head, :], jnp.max(dlog, axis=0, keepdims=True))
            p_t = (_dot_nt(kh, qh) * jnp.exp(dlog - m_t)).astype(BF16)
            v_t = jnp.where(lane == ML_DH, 1.0, v_ref[0, rows, cols].astype(F32)).T
            st = st_s[hd]
            isc = jnp.exp(inter_log[head, :] - m_t)
            acc = _dot(v_t.astype(BF16), p_t) + isc * _dot_nt(st.astype(BF16), qh)
            den = acc[ML_DH:ML_DH + 1, :]
            hv = acc[:ML_DH, :] * (1.0 / jnp.maximum(jnp.abs(den), jnp.exp(-m_t)))
            ss = jnp.sum(hv * hv, axis=0, keepdims=True) * (1.0 / ML_DH)
            hn_t = jnp.concatenate([hv * lax.rsqrt(ss + EPS), jnp.zeros((ML_DHP - ML_DH, ML_C), F32)], axis=0)
            hn = hn_t.T * og_ref[:, cols] * _sigmoid(o_ref[0, rows, cols].astype(F32))
            out_ref[0, rows, cols] = hn.astype(BF16)
            vw = (v_t * wexp[head, :]).astype(BF16)
            st_s[hd] = a_sc[head, :] * st + c_sc[head, :] * _dot(vw, kh)
        m_s[...] = m_new


def _mlstm(qk, v, o, gr1, gr2, gc1, gc2, cw, cb, gbr1, gbr2, gbc1, gbc2, og):
    B, L, _ = qk.shape
    tl = min(ML_TL, L)
    grid = (B, L // tl)
    tspec = lambda w_: pl.BlockSpec((1, tl, w_), lambda b, i: (b, i, 0))
    rspec = pl.BlockSpec((1, 8, tl), lambda b, i: (b, 0, i))
    return pl.pallas_call(
        functools.partial(_mlstm_kernel, tl=tl),
        grid=grid,
        in_specs=[tspec(2 * ML_WP), tspec(ML_WP), tspec(ML_WP), rspec, rspec, tspec(8), tspec(8),
                  _const_spec((ML_CONV, 2 * ML_WP)), _const_spec((1, 2 * ML_WP)),
                  _const_spec((8, 1)), _const_spec((8, 1)), _const_spec((1, 8)), _const_spec((1, 8)),
                  _const_spec((1, ML_WP))],
        out_specs=tspec(ML_WP),
        out_shape=jax.ShapeDtypeStruct((B, L, ML_WP), BF16),
        scratch_shapes=[pltpu.VMEM((tl + 8, 2 * ML_WP), F32),
                        pltpu.VMEM((ML_HEADS, ML_DHP, ML_DHP), F32),
                        pltpu.VMEM((8, 1), F32)],
        compiler_params=pltpu.CompilerParams(
            dimension_semantics=("parallel", "arbitrary"), vmem_limit_bytes=VMEM_LIMIT),
        name="mlstm",
    )(qk, v, o, gr1, gr2, gc1, gc2, cw, cb, gbr1, gbr2, gbc1, gbc2, og)


def _sb_kernel(stop_ref, q_ref, k_ref, v_ref, og_ref, out_ref, acc_s, r_s):
    i = pl.program_id(1)
    npair = SB_HEADS // 2
    lane = lax.broadcasted_iota(jnp.int32, (SB_TQ, 2 * SB_DH), 1)
    first = lane < SB_DH
    q_heads = []
    for p in range(npair):
        q = q_ref[0, :, p * 128:(p + 1) * 128]
        zero = jnp.zeros_like(q)
        q_heads.append(jnp.where(first, q, zero))
        q_heads.append(jnp.where(first, zero, q))
    ri = lax.broadcasted_iota(jnp.int32, (SB_TQ, SB_TK), 0)
    ci = lax.broadcasted_iota(jnp.int32, (SB_TQ, SB_TK), 1)
    rj = lax.broadcasted_iota(jnp.int32, (SB_TK, SB_TK), 0)
    cj = lax.broadcasted_iota(jnp.int32, (SB_TK, SB_TK), 1)
    suffix = jnp.where(rj >= cj, 1.0, 0.0).astype(BF16)
    sign = jnp.uint32(0x80000000)

    acc_s[...] = jnp.zeros_like(acc_s)
    r_s[...] = jnp.zeros_like(r_s)

    def blocks(tiles):
        units = [(t, hd) for t in range(len(tiles)) for hd in range(SB_HEADS)]
        ks, vs, zs, sbs = {}, {}, {}, {}
        for t, (j, _) in enumerate(tiles):
            start = pl.multiple_of(j * SB_TK, SB_TK)
            for p in range(npair):
                ks[t, p] = k_ref[0, pl.ds(start, SB_TK), p * 128:(p + 1) * 128]
                vs[t, p] = v_ref[0, pl.ds(start, SB_TK), p * 128:(p + 1) * 128]

        def scores(u):
            t, hd = u
            zs[u] = _dot_nt(q_heads[hd], ks[t, hd // 2])

        def suffix_sums(u):
            causal = tiles[u[0]][1]
            z = zs[u]
            neg_abs = lax.bitcast_convert_type(lax.bitcast_convert_type(z, jnp.uint32) | sign, F32)
            sp = jnp.maximum(z, 0.0) + jnp.log2(1.0 + jnp.exp2(neg_abs))
            if causal is not None:
                sp = jnp.where(causal, sp, 0.0)
            sbs[u] = _dot(sp.astype(BF16), suffix)

        def weights_and_values(u):
            t, hd = u
            causal = tiles[t][1]
            r = r_s[hd]
            a = jnp.exp2(zs[u] - (sbs[u] + r))
            if causal is not None:
                a = jnp.where(causal, a, 0.0)
            r_s[hd] = r + sbs[u][:, 0:1]
            acc_s[hd] += _dot(a.astype(BF16), vs[t, hd // 2])

        stages = (scores, suffix_sums, weights_and_values)
        for step in range(len(units) + len(stages) - 1):
            for lag, stage in enumerate(stages):
                if 0 <= step - lag < len(units):
                    stage(units[step - lag])

    nd = SB_TQ // SB_TK
    blocks([(i * nd + d, ci + d * SB_TK < ri) for d in reversed(range(nd))])

    def min_suffix():
        m = r_s[0]
        for hd in range(1, SB_HEADS):
            m = jnp.minimum(m, r_s[hd])
        return jnp.min(m)

    n_left = i * nd

    def cond(carry):
        t, rmin = carry
        return jnp.logical_and(t < n_left, rmin <= stop_ref[0])

    def body(carry):
        t, _ = carry
        blocks([(n_left - 1 - t, None)])
        return t + 1, min_suffix()

    lax.while_loop(cond, body, (jnp.int32(0), min_suffix()))

    for p in range(npair):
        o = jnp.where(first, acc_s[2 * p], acc_s[2 * p + 1])
        o2 = o * o
        ss_a = jnp.sum(jnp.where(first, o2, 0.0), axis=-1, keepdims=True)
        ss_b = jnp.sum(jnp.where(first, 0.0, o2), axis=-1, keepdims=True)
        ss = jnp.where(first, ss_a, ss_b) * (1.0 / SB_DH)
        out_ref[0, :, p * 128:(p + 1) * 128] = (
            o * lax.rsqrt(ss + EPS) * og_ref[:, p * 128:(p + 1) * 128]).astype(BF16)


def _sb_attn(stop, q, k, v, og):
    B, L, _ = q.shape
    assert L % SB_TQ == 0 and SB_TQ % SB_TK == 0
    grid = (B, L // SB_TQ)
    return pl.pallas_call(
        _sb_kernel,
        grid=grid,
        in_specs=[pl.BlockSpec(memory_space=pltpu.SMEM),
                  pl.BlockSpec((1, SB_TQ, SB_W), lambda b, i: (b, i, 0)),
                  pl.BlockSpec((1, L, SB_W), lambda b, i: (b, 0, 0)),
                  pl.BlockSpec((1, L, SB_W), lambda b, i: (b, 0, 0)),
                  _const_spec((1, SB_W))],
        out_specs=pl.BlockSpec((1, SB_TQ, SB_W), lambda b, i: (b, i, 0)),
        out_shape=jax.ShapeDtypeStruct((B, L, SB_W), BF16),
        scratch_shapes=[pltpu.VMEM((SB_HEADS, SB_TQ, 2 * SB_DH), F32),
                        pltpu.VMEM((SB_HEADS, SB_TQ, 1), F32)],
        compiler_params=pltpu.CompilerParams(
            dimension_semantics=("parallel", "parallel"), vmem_limit_bytes=VMEM_LIMIT),
        name="sb_attn",
    )(stop, q, k, v, og)


def _gelu_tanh(x):
    return 0.5 * x * (1.0 + jnp.tanh(math.sqrt(2.0 / math.pi) * (x + 0.044715 * (x * x * x))))


def _cmul(ar, ai, br, bi):
    return ar * br - ai * bi, ar * bi + ai * br


def _s5_kernel(u_ref, bm_ref, cm_ref, pw_ref, d_ref, gw_ref, gb_ref, og_ref, out_ref, xs_s, y_s, carry_s, *, tl):
    l = pl.program_id(1)
    nseg = tl // 8
    nlt = S5_LANES // 128
    grp = 8

    @pl.when(l == 0)
    def _():
        carry_s[...] = jnp.zeros_like(carry_s)

    u = jnp.concatenate([u_ref[0, hf] for hf in range(S5_W // 128)], axis=-1)
    xs_s[...] = _dot(u.astype(BF16), bm_ref[...])

    def lanes(lt):
        return slice(lt * 128, (lt + 1) * 128), slice(S5_LANES + lt * 128, S5_LANES + (lt + 1) * 128)

    for g0 in range(0, nlt, grp):
        tiles = range(g0, g0 + grp)
        lam = [(pw_ref[0, :, lanes(lt)[0]], pw_ref[1, :, lanes(lt)[0]]) for lt in tiles]

        def local_step(j, st):
            r0 = pl.multiple_of(j * 8, 8)
            new = []
            for n, lt in enumerate(tiles):
                re_l, im_l = lanes(lt)
                pr, pi_ = _cmul(lam[n][0], lam[n][1], st[2 * n], st[2 * n + 1])
                xr = xs_s[pl.ds(r0, 8), re_l] + pr
                xi = xs_s[pl.ds(r0, 8), im_l] + pi_
                xs_s[pl.ds(r0, 8), re_l] = xr
                xs_s[pl.ds(r0, 8), im_l] = xi
                new += [xr, xi]
            return tuple(new)

        fin = lax.fori_loop(0, nseg, local_step, tuple(jnp.zeros((8, 128), F32) for _ in range(2 * grp)))

        start = []
        for n, lt in enumerate(tiles):
            re_l, im_l = lanes(lt)
            er, ei = fin[2 * n], fin[2 * n + 1]
            for m, k in enumerate((1, 2, 4)):
                pr, pi_ = _cmul(pw_ref[2 + 2 * m, :, re_l], pw_ref[3 + 2 * m, :, re_l],
                                pltpu.roll(er, k, 0), pltpu.roll(ei, k, 0))
                er, ei = er + pr, ei + pi_
            s0r, s0i = carry_s[:, re_l], carry_s[:, im_l]
            pr, pi_ = _cmul(pw_ref[8, :, re_l], pw_ref[9, :, re_l], s0r, s0i)
            er, ei = er + pr, ei + pi_
            carry_s[:, re_l] = jnp.broadcast_to(er[7:8, :], (8, 128))
            carry_s[:, im_l] = jnp.broadcast_to(ei[7:8, :], (8, 128))
            row = lax.broadcasted_iota(jnp.int32, (8, 128), 0)
            start += [jnp.where(row == 0, s0r, pltpu.roll(er, 1, 0)),
                      jnp.where(row == 0, s0i, pltpu.roll(ei, 1, 0))]

        def fix_step(j, tr):
            r0 = pl.multiple_of(j * 8, 8)
            new = []
            for n, lt in enumerate(tiles):
                re_l, im_l = lanes(lt)
                pr, pi_ = _cmul(lam[n][0], lam[n][1], tr[2 * n], tr[2 * n + 1])
                xs_s[pl.ds(r0, 8), re_l] += pr
                xs_s[pl.ds(r0, 8), im_l] += pi_
                new += [pr, pi_]
            return tuple(new)

        lax.fori_loop(0, nseg, fix_step, tuple(start))

    rb = min(256, tl)
    for r0 in range(0, tl, rb):
        rows = slice(r0, r0 + rb)
        y = _dot(xs_s[rows, :].astype(BF16), cm_ref[...]) + d_ref[...] * u[rows, :]
        y = _gelu_tanh(y)
        y = y * _sigmoid(_dot(y.astype(BF16), gw_ref[...]) + gb_ref[...])
        ms = jnp.mean(y * y, axis=-1, keepdims=True)
        y = y * lax.rsqrt(ms + EPS) * og_ref[...]
        for hf in range(S5_W // 128):
            y_s[hf, rows, :] = y[:, hf * 128:(hf + 1) * 128]
    for r in range(8):
        out_ref[0, r * nseg:(r + 1) * nseg, :] = jnp.concatenate(
            [y_s[hf, pl.ds(r, nseg, stride=8), :] for hf in range(S5_W // 128)], axis=-1).astype(BF16)


def _s5(u, bm, cm, pw, d, gw, gb, og):
    B, _, L, _ = u.shape
    tl = min(S5_TL, L)
    grid = (B, L // tl)
    tspec = pl.BlockSpec((1, tl, S5_W), lambda b, i: (b, i, 0))
    uspec = pl.BlockSpec((1, S5_W // 128, tl, 128), lambda b, i: (b, 0, i, 0))
    return pl.pallas_call(
        functools.partial(_s5_kernel, tl=tl),
        grid=grid,
        in_specs=[uspec, _const_spec((S5_W, 2 * S5_LANES)), _const_spec((2 * S5_LANES, S5_W)),
                  _const_spec((10, 8, S5_LANES)), _const_spec((1, S5_W)), _const_spec((S5_W, S5_W)),
                  _const_spec((1, S5_W)), _const_spec((1, S5_W))],
        out_specs=tspec,
        out_shape=jax.ShapeDtypeStruct((B, L, S5_W), BF16),
        scratch_shapes=[pltpu.VMEM((tl, 2 * S5_LANES), F32), pltpu.VMEM((S5_W // 128, tl, 128), F32),
                        pltpu.VMEM((8, 2 * S5_LANES), F32)],
        compiler_params=pltpu.CompilerParams(
            dimension_semantics=("parallel", "arbitrary"), vmem_limit_bytes=VMEM_LIMIT),
        name="s5",
    )(u, bm, cm, pw, d, gw, gb, og)


def _ffn_kernel(x_ref, hm_ref, hs_ref, h5_ref, wom_ref, wos_ref, wo5_ref, g_ref,
                wg_ref, wv_ref, cw_ref, wd_ref, out_ref, h_s, gext_s, act_s, carry_s, *, tm):
    l = pl.program_id(1)

    @pl.when(l == 0)
    def _():
        carry_s[...] = jnp.zeros_like(carry_s)

    x1 = (x_ref[0] + _dot(hm_ref[0], wom_ref[...]) + _dot(hs_ref[0], wos_ref[...])
          + _dot(h5_ref[0], wo5_ref[...]))
    out_ref[0] = x1
    ms = jnp.mean(x1 * x1, axis=-1, keepdims=True)
    h_s[...] = (x1 * lax.rsqrt(ms + EPS) * g_ref[...]).astype(BF16)

    for c in range(D_FF // FFN_FC):
        cols = slice(c * FFN_FC, (c + 1) * FFN_FC)
        g = _dot(h_s[...], wg_ref[:, cols])
        v = _dot(h_s[...], wv_ref[:, cols])
        gext_s[0:8, :] = carry_s[c]
        gext_s[8:8 + tm, :] = g
        carry_s[c] = g[tm - 8:tm, :]
        conv = (cw_ref[2:3, cols] * g + cw_ref[1:2, cols] * gext_s[7:7 + tm, :]
                + cw_ref[0:1, cols] * gext_s[6:6 + tm, :])
        act_s[:, cols] = (conv * _sigmoid(conv) * v).astype(BF16)

    out_ref[0] += _dot(act_s[...], wd_ref[...])


def _ffn(x, hm, hs, h5, wom, wos, wo5, g, wg, wv, cw, wd):
    B, L, _ = x.shape
    tm = min(FFN_TM, L)
    grid = (B, L // tm)
    tspec = lambda w_: pl.BlockSpec((1, tm, w_), lambda b, i: (b, i, 0))
    return pl.pallas_call(
        functools.partial(_ffn_kernel, tm=tm),
        grid=grid,
        in_specs=[tspec(D_MODEL), tspec(ML_WP), tspec(SB_W), tspec(S5_W),
                  _const_spec((ML_WP, D_MODEL)), _const_spec((SB_W, D_MODEL)), _const_spec((S5_W, D_MODEL)),
                  _const_spec((1, D_MODEL)), _const_spec((D_MODEL, D_FF)), _const_spec((D_MODEL, D_FF)),
                  _const_spec((3, D_FF)), _const_spec((D_FF, D_MODEL))],
        out_specs=tspec(D_MODEL),
        out_shape=jax.ShapeDtypeStruct((B, L, D_MODEL), F32),
        scratch_shapes=[pltpu.VMEM((tm, D_MODEL), BF16), pltpu.VMEM((tm + 8, FFN_FC), F32),
                        pltpu.VMEM((tm, D_FF), BF16), pltpu.VMEM((D_FF // FFN_FC, 8, FFN_FC), F32)],
        compiler_params=pltpu.CompilerParams(
            dimension_semantics=("parallel", "arbitrary"), vmem_limit_bytes=VMEM_LIMIT),
        name="ffn",
    )(x, hm, hs, h5, wom, wos, wo5, g, wg, wv, cw, wd)


def _pad_heads(w):
    lead = w.shape[:-1]
    w = w.reshape(lead + (ML_HEADS, ML_DH))
    w = jnp.pad(w, [(0, 0)] * len(lead) + [(0, 0), (0, ML_DHP - ML_DH)])
    return w.reshape(lead + (ML_WP,))


def _pad_lanes(w, n):
    return jnp.pad(w, [(0, 0)] * (w.ndim - 1) + [(0, n - w.shape[-1])])


def _s5_params(a_re, a_im, log_dt, b_re, b_im, c_re, c_im, nseg):
    lr = jnp.minimum(a_re, -1e-4)
    li = a_im
    dt = jnp.exp(log_dt)[:, None]
    mag = jnp.exp(lr * dt)
    br = mag * jnp.cos(li * dt)
    bi = mag * jnp.sin(li * dt)
    nr, ni = br - 1.0, bi
    den = lr * lr + li * li
    fr = (nr * lr + ni * li) / den
    fi = (ni * lr - nr * li) / den
    bbr = fr[..., None] * b_re - fi[..., None] * b_im
    bbi = fr[..., None] * b_im + fi[..., None] * b_re
    eye = jnp.eye(S5_GROUPS, dtype=F32)
    bm_re = jnp.einsum("gk,gph->ghkp", eye, bbr).reshape(S5_W, S5_LANES)
    bm_im = jnp.einsum("gk,gph->ghkp", eye, bbi).reshape(S5_W, S5_LANES)
    bm = jnp.concatenate([bm_re, bm_im], axis=1).astype(BF16)
    cm_re = jnp.einsum("gk,ghp->gpkh", eye, c_re).reshape(S5_LANES, S5_W)
    cm_im = jnp.einsum("gk,ghp->gpkh", eye, c_im).reshape(S5_LANES, S5_W)
    cm = jnp.concatenate([cm_re, -cm_im], axis=0).astype(BF16)
    def cpow2(r, i, n):
        for _ in range(n):
            r, i = r * r - i * i, 2.0 * r * i
        return r, i

    lr_, li_ = br.reshape(-1), bi.reshape(-1)
    assert nseg & (nseg - 1) == 0
    ar, ai = cpow2(lr_, li_, nseg.bit_length() - 1)
    pr, pi_ = [ar], [ai]
    for _ in range(7):
        r, i = pr[-1], pi_[-1]
        pr.append(r * ar - i * ai)
        pi_.append(r * ai + i * ar)
    row = jnp.arange(8)[:, None]
    planes = [jnp.broadcast_to(lr_[None, :], (8, S5_LANES)), jnp.broadcast_to(li_[None, :], (8, S5_LANES))]
    for k in (1, 2, 4):
        keep = row >= k
        planes.append(jnp.where(keep, pr[k - 1][None, :], 0.0))
        planes.append(jnp.where(keep, pi_[k - 1][None, :], 0.0))
    planes.append(jnp.stack(pr, axis=0))
    planes.append(jnp.stack(pi_, axis=0))
    pw = jnp.stack(planes, axis=0).astype(F32)
    return bm, cm, pw


def kernel(x, norm_mix_g, w_in, ml_conv_w, ml_conv_b, ml_gate_b, ml_out_g, sb_q_g, sb_k_g, sb_out_g, s5_a_re, s5_a_im, s5_log_dt, s5_b_re, s5_b_im, s5_c_re, s5_c_im, s5_d, s5_glu_w, s5_glu_b, s5_out_g, w_out, norm_ffn_g, ffn_w_up, ffn_conv_w, ffn_w_down):
    depth = w_in.shape[0]
    seg = jnp.kron(jnp.eye(SB_HEADS, dtype=F32), jnp.ones((SB_DH, SB_DH), F32)).astype(BF16)
    for l in range(depth):
        w = w_in[l]
        o0 = 0
        parts = []
        for wd_ in (ML_W, ML_W, ML_W, ML_W, 2 * ML_HEADS, SB_W, SB_W, SB_W, S5_W):
            parts.append(w[:, o0:o0 + wd_])
            o0 += wd_
        wq, wk, wv_, wo_, wif, wsq, wsk, wsv, wu = parts
        wfi = jnp.concatenate([wif[:, ML_HEADS:], wif[:, :ML_HEADS]], axis=1)
        w_all = jnp.concatenate(
            [_pad_heads(wq), _pad_heads(wk), _pad_heads(wv_), _pad_heads(wo_), wsq, wsk, wsv, wu,
             _pad_lanes(wif, 128), _pad_lanes(wfi, 128)], axis=1).astype(BF16)
        qg = jnp.tile(sb_q_g[l], SB_HEADS)[None, :] * (SB_DH ** -0.5 * math.log2(math.e))
        kg = jnp.tile(sb_k_g[l], SB_HEADS)[None, :]
        (qk, mv, mo, sq, sk, sv, u, gr1, gr2, gc1, gc2) = _in_proj(
            x, norm_mix_g[l][None, :], w_all, seg, qg, kg)

        cw = jnp.concatenate([_pad_heads(ml_conv_w[l][:, :ML_W]), _pad_heads(ml_conv_w[l][:, ML_W:])], axis=1)
        cb = jnp.concatenate([_pad_heads(ml_conv_b[l][:ML_W]), _pad_heads(ml_conv_b[l][ML_W:])])[None, :]
        gb = ml_gate_b[l]
        gb2 = jnp.concatenate([gb[ML_HEADS:], gb[:ML_HEADS]])
        hm = _mlstm(qk, mv, mo, gr1, gr2, gc1, gc2, cw, cb, gb[:, None], gb2[:, None], gb[None, :],
                    gb2[None, :], _pad_heads(ml_out_g[l].reshape(-1))[None, :])

        stop = (1.05 * SB_DH * jnp.max(jnp.abs(qg)) * jnp.max(jnp.abs(kg)) + 160.0).reshape(1).astype(F32)
        hs = _sb_attn(stop, sq, sk, sv, sb_out_g[l].reshape(1, SB_W))

        bm, cm, pw = _s5_params(s5_a_re[l], s5_a_im[l], s5_log_dt[l], s5_b_re[l], s5_b_im[l],
                                s5_c_re[l], s5_c_im[l], min(S5_TL, x.shape[1]) // 8)
        h5 = _s5(u, bm, cm, pw, s5_d[l].reshape(1, S5_W), s5_glu_w[l].astype(BF16),
                 s5_glu_b[l][None, :], s5_out_g[l][None, :])

        wo = w_out[l]
        wom = jnp.pad(wo[:ML_W].reshape(ML_HEADS, ML_DH, D_MODEL),
                      [(0, 0), (0, ML_DHP - ML_DH), (0, 0)]).reshape(ML_WP, D_MODEL).astype(BF16)
        wos = wo[ML_W:ML_W + SB_W].astype(BF16)
        wo5 = wo[ML_W + SB_W:].astype(BF16)
        x = _ffn(x, hm, hs, h5, wom, wos, wo5, norm_ffn_g[l][None, :],
                 ffn_w_up[l][:, :D_FF].astype(BF16), ffn_w_up[l][:, D_FF:].astype(BF16),
                 ffn_conv_w[l], ffn_w_down[l].astype(BF16))
    return x
```

```python
import functools
import math

import jax
import jax.numpy as jnp
from jax import lax
from jax.experimental import pallas as pl
from jax.experimental.pallas import tpu as pltpu

F32 = jnp.float32
BF16 = jnp.bfloat16

D_MODEL = 1024
ML_HEADS = 4
ML_DH = 96
ML_DHP = 128
ML_WP = ML_HEADS * ML_DHP
ML_W = ML_HEADS * ML_DH
ML_CONV = 4
SB_HEADS = 6
SB_DH = 64
SB_W = SB_HEADS * SB_DH
S5_W = 256
S5_GROUPS = 16
S5_CH = 16
S5_STATE = 64
S5_LANES = S5_GROUPS * S5_STATE
D_FF = 2816
EPS = 1e-6
NEG = -1e30

C_QK, C_V, C_O, C_SQ, C_SK, C_SV, C_U, C_END = (0, 1024, 1536, 2048, 2432, 2816, 3200, 3456)
GATE_LANE = 104

VMEM_LIMIT = 56 * 1024 * 1024

IN_TM = 1024
ML_TL = 512
ML_C = 128
SB_TQ = 256
SB_TK = 256
S5_TL = 1024
FFN_TM = 512
FFN_FC = 256


def _const_spec(shape):
    nd = len(shape)
    return pl.BlockSpec(shape, lambda *_: (0,) * nd, pipeline_mode=pl.Buffered(1))


def _dot(a, b):
    return jnp.dot(a, b, preferred_element_type=F32)


def _dot_nt(a, b):
    return lax.dot_general(a, b, (((1,), (1,)), ((), ())), preferred_element_type=F32)


def _dot_tn(a, b):
    return lax.dot_general(a, b, (((0,), (0,)), ((), ())), preferred_element_type=F32)


def _split(x):
    hi = x.astype(BF16)
    lo = (x - hi.astype(F32)).astype(BF16)
    return hi, lo


def _log_sigmoid(x):
    return jnp.minimum(x, 0.0) - jnp.log(1.0 + jnp.exp(-jnp.abs(x)))


def _sigmoid(x):
    return 1.0 / (1.0 + jnp.exp(-x))


def _in_proj_kernel(x_ref, g_ref, w_ref, seg_ref, qg_ref, kg_ref,
                    qk_ref, v_ref, o_ref, sq_ref, sk_ref, sv_ref, u_ref,
                    gr1_ref, gr2_ref, gc1_ref, gc2_ref):
    x = x_ref[0]
    ms = jnp.mean(x * x, axis=-1, keepdims=True)
    h = (x * lax.rsqrt(ms + EPS) * g_ref[...]).astype(BF16)

    def proj(a, b):
        return _dot(h, w_ref[:, a:b])

    def qk_norm(z, gain):
        ss = _dot((z * z).astype(BF16), seg_ref[...])
        return (z * lax.rsqrt(ss * (1.0 / SB_DH) + EPS) * gain).astype(BF16)

    qk_ref[0] = proj(C_QK, C_V)
    zv = proj(C_V, C_O)
    v_ref[0] = zv.astype(BF16)
    o_ref[0] = proj(C_O, C_SQ).astype(BF16)
    sq_ref[0] = qk_norm(proj(C_SQ, C_SK), qg_ref[...])
    sk_ref[0] = qk_norm(proj(C_SK, C_SV), kg_ref[...])
    sv_ref[0] = proj(C_SV, C_U).astype(BF16)
    zu = proj(C_U, C_END)
    nseg = zu.shape[0] // 8
    for r in range(8):
        for hf in range(S5_W // 128):
            u_ref[0, hf, pl.ds(r, nseg, stride=8), :] = zu[r * nseg:(r + 1) * nseg, hf * 128:(hf + 1) * 128]
    gates = slice(GATE_LANE, GATE_LANE + 8)
    z1 = zv[:, :ML_DHP]
    z2 = zv[:, ML_DHP:2 * ML_DHP]
    gc1_ref[0] = z1[:, gates]
    gc2_ref[0] = z2[:, gates]
    gr1_ref[0] = z1.T[gates, :]
    gr2_ref[0] = z2.T[gates, :]


def _in_proj(x, g, w, seg, qg, kg):
    B, L, _ = x.shape
    tm = min(IN_TM, L)
    grid = (B, L // tm)
    tok = lambda w_, dt: jax.ShapeDtypeStruct((B, L, w_), dt)
    tspec = lambda w_: pl.BlockSpec((1, tm, w_), lambda b, i: (b, i, 0))
    rspec = pl.BlockSpec((1, 8, tm), lambda b, i: (b, 0, i))
    return pl.pallas_call(
        _in_proj_kernel,
        grid=grid,
        in_specs=[tspec(D_MODEL), _const_spec((1, D_MODEL)), _const_spec((D_MODEL, C_END)),
                  _const_spec((SB_W, SB_W)), _const_spec((1, SB_W)), _const_spec((1, SB_W))],
        out_specs=[tspec(2 * ML_WP), tspec(ML_WP), tspec(ML_WP), tspec(SB_W), tspec(SB_W), tspec(SB_W),
                   pl.BlockSpec((1, S5_W // 128, tm, 128), lambda b, i: (b, 0, i, 0)),
                   rspec, rspec, tspec(8), tspec(8)],
        out_shape=[tok(2 * ML_WP, F32), tok(ML_WP, BF16), tok(ML_WP, BF16), tok(SB_W, BF16),
                   tok(SB_W, BF16), tok(SB_W, BF16), jax.ShapeDtypeStruct((B, S5_W // 128, L, 128), F32),
                   jax.ShapeDtypeStruct((B, 8, L), F32), jax.ShapeDtypeStruct((B, 8, L), F32),
                   tok(8, F32), tok(8, F32)],
        compiler_params=pltpu.CompilerParams(
            dimension_semantics=("parallel", "parallel"), vmem_limit_bytes=VMEM_LIMIT),
        name="in_proj",
    )(x, g, w, seg, qg, kg)


def _mlstm_kernel(qk_ref, v_ref, o_ref, gr1_ref, gr2_ref, gc1_ref, gc2_ref,
                  cw_ref, cb_ref, gbr1_ref, gbr2_ref, gbc1_ref, gbc2_ref, og_ref,
                  out_ref, ext_s, st_s, m_s, *, tl):
    l = pl.program_id(1)

    @pl.when(l == 0)
    def _():
        ext_s[0:8, :] = jnp.zeros((8, 2 * ML_WP), F32)
        st_s[...] = jnp.zeros_like(st_s)
        m_s[...] = jnp.zeros_like(m_s)

    pre = qk_ref[0]
    ext_s[8:8 + tl, :] = pre
    conv = (cw_ref[3:4, :] * pre + cw_ref[2:3, :] * ext_s[7:7 + tl, :]
            + cw_ref[1:2, :] * ext_s[6:6 + tl, :] + cw_ref[0:1, :] * ext_s[5:5 + tl, :] + cb_ref[...])
    ext_s[0:8, :] = pre[tl - 8:tl, :]
    qk = conv * _sigmoid(conv)
    q_all = qk[:, :ML_WP].astype(BF16)
    k_all = qk[:, ML_WP:] * (ML_DH ** -0.5)

    gr1 = gr1_ref[0] + gbr1_ref[...]
    gr2 = gr2_ref[0] + gbr2_ref[...]
    gc1 = gc1_ref[0] + gbc1_ref[...]
    gc2 = gc2_ref[0] + gbc2_ref[...]

    ri = lax.broadcasted_iota(jnp.int32, (ML_C, ML_C), 0)
    ci = lax.broadcasted_iota(jnp.int32, (ML_C, ML_C), 1)
    visible = ri <= ci
    tri_low = jnp.where(ci <= ri, 1.0, 0.0).astype(BF16)
    tri_up = jnp.where(visible, 1.0, 0.0).astype(BF16)
    lane = lax.broadcasted_iota(jnp.int32, (ML_C, ML_DHP), 1)

    for c in range(tl // ML_C):
        rows = slice(c * ML_C, (c + 1) * ML_C)
        lf_hi, lf_lo = _split(_log_sigmoid(gc2[rows, :]))
        r_col = gc1[rows, :] - (_dot(tri_low, lf_hi) + _dot(tri_low, lf_lo))
        ir = gr1[:, rows]
        lr_hi, lr_lo = _split(_log_sigmoid(gr2[:, rows]))
        b_row = _dot(lr_hi, tri_up) + _dot(lr_lo, tri_up)
        btot = b_row[:, ML_C - 1:ML_C]
        mp = m_s[...]
        wlog = btot - b_row + ir
        m_loc = jnp.max(wlog, axis=-1, keepdims=True)
        wexp = jnp.exp(wlog - m_loc)
        m_new = jnp.maximum(btot + mp, m_loc)
        a_sc = jnp.exp(btot + mp - m_new)
        c_sc = jnp.exp(m_loc - m_new)
        inter_log = b_row + mp

        for hd in range(ML_HEADS):
            cols = slice(hd * ML_DHP, (hd + 1) * ML_DHP)
            head = slice(hd, hd + 1)
            qh = q_all[rows, cols]
            kh = k_all[rows, cols].astype(BF16)
            dlog = jnp.where(visible, r_col[:, head] + b_row[head, :], NEG)
            m_t = jnp.maximum(inter_log[head, :], jnp.max(dlog, axis=0, keepdims=True))
            p_t = (_dot_nt(kh, qh) * jnp.exp(dlog - m_t)).astype(BF16)
            v_t = jnp.where(lane == ML_DH, 1.0, v_ref[0, rows, cols].astype(F32)).T
            st = st_s[hd]
            isc = jnp.exp(inter_log[head, :] - m_t)
            acc = _dot(v_t.astype(BF16), p_t) + isc * _dot_nt(st.astype(BF16), qh)
            den = acc[ML_DH:ML_DH + 1, :]
            hv = acc[:ML_DH, :] * (1.0 / jnp.maximum(jnp.abs(den), jnp.exp(-m_t)))
            ss = jnp.sum(hv * hv, axis=0, keepdims=True) * (1.0 / ML_DH)
            hn_t = jnp.concatenate([hv * lax.rsqrt(ss + EPS), jnp.zeros((ML_DHP - ML_DH, ML_C), F32)], axis=0)
            hn = hn_t.T * og_ref[:, cols] * _sigmoid(o_ref[0, rows, cols].astype(F32))
            out_ref[0, rows, cols] = hn.astype(BF16)
            vw = (v_t * wexp[head, :]).astype(BF16)
            st_s[hd] = a_sc[head, :] * st + c_sc[head, :] * _dot(vw, kh)
        m_s[...] = m_new


def _mlstm(qk, v, o, gr1, gr2, gc1, gc2, cw, cb, gbr1, gbr2, gbc1, gbc2, og):
    B, L, _ = qk.shape
    tl = min(ML_TL, L)
    grid = (B, L // tl)
    tspec = lambda w_: pl.BlockSpec((1, tl, w_), lambda b, i: (b, i, 0))
    rspec = pl.BlockSpec((1, 8, tl), lambda b, i: (b, 0, i))
    return pl.pallas_call(
        functools.partial(_mlstm_kernel, tl=tl),
        grid=grid,
        in_specs=[tspec(2 * ML_WP), tspec(ML_WP), tspec(ML_WP), rspec, rspec, tspec(8), tspec(8),
                  _const_spec((ML_CONV, 2 * ML_WP)), _const_spec((1, 2 * ML_WP)),
                  _const_spec((8, 1)), _const_spec((8, 1)), _const_spec((1, 8)), _const_spec((1, 8)),
                  _const_spec((1, ML_WP))],
        out_specs=tspec(ML_WP),
        out_shape=jax.ShapeDtypeStruct((B, L, ML_WP), BF16),
        scratch_shapes=[pltpu.VMEM((tl + 8, 2 * ML_WP), F32),
                        pltpu.VMEM((ML_HEADS, ML_DHP, ML_DHP), F32),
                        pltpu.VMEM((8, 1), F32)],
        compiler_params=pltpu.CompilerParams(
            dimension_semantics=("parallel", "arbitrary"), vmem_limit_bytes=VMEM_LIMIT),
        name="mlstm",
    )(qk, v, o, gr1, gr2, gc1, gc2, cw, cb, gbr1, gbr2, gbc1, gbc2, og)


def _sb_kernel(stop_ref, q_ref, k_ref, v_ref, og_ref, out_ref, acc_s, r_s):
    i = pl.program_id(1)
    npair = SB_HEADS // 2
    lane = lax.broadcasted_iota(jnp.int32, (SB_TQ, 2 * SB_DH), 1)
    first = lane < SB_DH
    q_heads = []
    for p in range(npair):
        q = q_ref[0, :, p * 128:(p + 1) * 128]
        zero = jnp.zeros_like(q)
        q_heads.append(jnp.where(first, q, zero))
        q_heads.append(jnp.where(first, zero, q))
    ri = lax.broadcasted_iota(jnp.int32, (SB_TQ, SB_TK), 0)
    ci = lax.broadcasted_iota(jnp.int32, (SB_TQ, SB_TK), 1)
    rj = lax.broadcasted_iota(jnp.int32, (SB_TK, SB_TK), 0)
    cj = lax.broadcasted_iota(jnp.int32, (SB_TK, SB_TK), 1)
    suffix = jnp.where(rj >= cj, 1.0, 0.0).astype(BF16)
    sign = jnp.uint32(0x80000000)

    acc_s[...] = jnp.zeros_like(acc_s)
    r_s[...] = jnp.zeros_like(r_s)

    def blocks(tiles):
        units = [(t, hd) for t in range(len(tiles)) for hd in range(SB_HEADS)]
        ks, vs, zs, sbs = {}, {}, {}, {}
        for t, (j, _) in enumerate(tiles):
            start = pl.multiple_of(j * SB_TK, SB_TK)
            for p in range(npair):
                ks[t, p] = k_ref[0, pl.ds(start, SB_TK), p * 128:(p + 1) * 128]
                vs[t, p] = v_ref[0, pl.ds(start, SB_TK), p * 128:(p + 1) * 128]

        def scores(u):
            t, hd = u
            zs[u] = _dot_nt(q_heads[hd], ks[t, hd // 2])

        def suffix_sums(u):
            causal = tiles[u[0]][1]
            z = zs[u]
            neg_abs = lax.bitcast_convert_type(lax.bitcast_convert_type(z, jnp.uint32) | sign, F32)
            sp = jnp.maximum(z, 0.0) + jnp.log2(1.0 + jnp.exp2(neg_abs))
            if causal is not None:
                sp = jnp.where(causal, sp, 0.0)
            sbs[u] = _dot(sp.astype(BF16), suffix)

        def weights_and_values(u):
            t, hd = u
            causal = tiles[t][1]
            r = r_s[hd]
            a = jnp.exp2(zs[u] - (sbs[u] + r))
            if causal is not None:
                a = jnp.where(causal, a, 0.0)
            r_s[hd] = r + sbs[u][:, 0:1]
            acc_s[hd] += _dot(a.astype(BF16), vs[t, hd // 2])

        stages = (scores, suffix_sums, weights_and_values)
        for step in range(len(units) + len(stages) - 1):
            for lag, stage in enumerate(stages):
                if 0 <= step - lag < len(units):
                    stage(units[step - lag])

    nd = SB_TQ // SB_TK
    blocks([(i * nd + d, ci + d * SB_TK < ri) for d in reversed(range(nd))])

    def min_suffix():
        m = r_s[0]
        for hd in range(1, SB_HEADS):
            m = jnp.minimum(m, r_s[hd])
        return jnp.min(m)

    n_left = i * nd

    def cond(carry):
        t, rmin = carry
        return jnp.logical_and(t < n_left, rmin <= stop_ref[0])

    def body(carry):
        t, _ = carry
        blocks([(n_left - 1 - t, None)])
        return t + 1, min_suffix()

    lax.while_loop(cond, body, (jnp.int32(0), min_suffix()))

    for p in range(npair):
        o = jnp.where(first, acc_s[2 * p], acc_s[2 * p + 1])
        o2 = o * o
        ss_a = jnp.sum(jnp.where(first, o2, 0.0), axis=-1, keepdims=True)
        ss_b = jnp.sum(jnp.where(first, 0.0, o2), axis=-1, keepdims=True)
        ss = jnp.where(first, ss_a, ss_b) * (1.0 / SB_DH)
        out_ref[0, :, p * 128:(p + 1) * 128] = (
            o * lax.rsqrt(ss + EPS) * og_ref[:, p * 128:(p + 1) * 128]).astype(BF16)


def _sb_attn(stop, q, k, v, og):
    B, L, _ = q.shape
    assert L % SB_TQ == 0 and SB_TQ % SB_TK == 0
    grid = (B, L // SB_TQ)
    return pl.pallas_call(
        _sb_kernel,
        grid=grid,
        in_specs=[pl.BlockSpec(memory_space=pltpu.SMEM),
                  pl.BlockSpec((1, SB_TQ, SB_W), lambda b, i: (b, i, 0)),
                  pl.BlockSpec((1, L, SB_W), lambda b, i: (b, 0, 0)),
                  pl.BlockSpec((1, L, SB_W), lambda b, i: (b, 0, 0)),
                  _const_spec((1, SB_W))],
        out_specs=pl.BlockSpec((1, SB_TQ, SB_W), lambda b, i: (b, i, 0)),
        out_shape=jax.ShapeDtypeStruct((B, L, SB_W), BF16),
        scratch_shapes=[pltpu.VMEM((SB_HEADS, SB_TQ, 2 * SB_DH), F32),
                        pltpu.VMEM((SB_HEADS, SB_TQ, 1), F32)],
        compiler_params=pltpu.CompilerParams(
            dimension_semantics=("parallel", "parallel"), vmem_limit_bytes=VMEM_LIMIT),
        name="sb_attn",
    )(stop, q, k, v, og)


def _gelu_tanh(x):
    return 0.5 * x * (1.0 + jnp.tanh(math.sqrt(2.0 / math.pi) * (x + 0.044715 * (x * x * x))))


def _cmul(ar, ai, br, bi):
    return ar * br - ai * bi, ar * bi + ai * br


def _s5_kernel(u_ref, bm_ref, cm_ref, pw_ref, d_ref, gw_ref, gb_ref, og_ref, out_ref, xs_s, y_s, carry_s, *, tl):
    l = pl.program_id(1)
    nseg = tl // 8
    nlt = S5_LANES // 128
    grp = 8

    @pl.when(l == 0)
    def _():
        carry_s[...] = jnp.zeros_like(carry_s)

    u = jnp.concatenate([u_ref[0, hf] for hf in range(S5_W // 128)], axis=-1)
    xs_s[...] = _dot(u.astype(BF16), bm_ref[...])

    def lanes(lt):
        return slice(lt * 128, (lt + 1) * 128), slice(S5_LANES + lt * 128, S5_LANES + (lt + 1) * 128)

    for g0 in range(0, nlt, grp):
        tiles = range(g0, g0 + grp)
        lam = [(pw_ref[0, :, lanes(lt)[0]], pw_ref[1, :, lanes(lt)[0]]) for lt in tiles]

        def local_step(j, st):
            r0 = pl.multiple_of(j * 8, 8)
            new = []
            for n, lt in enumerate(tiles):
                re_l, im_l = lanes(lt)
                pr, pi_ = _cmul(lam[n][0], lam[n][1], st[2 * n], st[2 * n + 1])
                xr = xs_s[pl.ds(r0, 8), re_l] + pr
                xi = xs_s[pl.ds(r0, 8), im_l] + pi_
                xs_s[pl.ds(r0, 8), re_l] = xr
                xs_s[pl.ds(r0, 8), im_l] = xi
                new += [xr, xi]
            return tuple(new)

        fin = lax.fori_loop(0, nseg, local_step, tuple(jnp.zeros((8, 128), F32) for _ in range(2 * grp)),
                            unroll=4)

        start = []
        for n, lt in enumerate(tiles):
            re_l, im_l = lanes(lt)
            er, ei = fin[2 * n], fin[2 * n + 1]
            for m, k in enumerate((1, 2, 4)):
                pr, pi_ = _cmul(pw_ref[2 + 2 * m, :, re_l], pw_ref[3 + 2 * m, :, re_l],
                                pltpu.roll(er, k, 0), pltpu.roll(ei, k, 0))
                er, ei = er + pr, ei + pi_
            s0r, s0i = carry_s[:, re_l], carry_s[:, im_l]
            pr, pi_ = _cmul(pw_ref[8, :, re_l], pw_ref[9, :, re_l], s0r, s0i)
            er, ei = er + pr, ei + pi_
            carry_s[:, re_l] = jnp.broadcast_to(er[7:8, :], (8, 128))
            carry_s[:, im_l] = jnp.broadcast_to(ei[7:8, :], (8, 128))
            row = lax.broadcasted_iota(jnp.int32, (8, 128), 0)
            start += [jnp.where(row == 0, s0r, pltpu.roll(er, 1, 0)),
                      jnp.where(row == 0, s0i, pltpu.roll(ei, 1, 0))]

        def fix_step(j, tr):
            r0 = pl.multiple_of(j * 8, 8)
            new = []
            for n, lt in enumerate(tiles):
                re_l, im_l = lanes(lt)
                pr, pi_ = _cmul(lam[n][0], lam[n][1], tr[2 * n], tr[2 * n + 1])
                xs_s[pl.ds(r0, 8), re_l] += pr
                xs_s[pl.ds(r0, 8), im_l] += pi_
                new += [pr, pi_]
            return tuple(new)

        lax.fori_loop(0, nseg, fix_step, tuple(start), unroll=4)

    rb = min(256, tl)
    for r0 in range(0, tl, rb):
        rows = slice(r0, r0 + rb)
        y = _dot(xs_s[rows, :].astype(BF16), cm_ref[...]) + d_ref[...] * u[rows, :]
        y = _gelu_tanh(y)
        y = y * _sigmoid(_dot(y.astype(BF16), gw_ref[...]) + gb_ref[...])
        ms = jnp.mean(y * y, axis=-1, keepdims=True)
        y = y * lax.rsqrt(ms + EPS) * og_ref[...]
        for hf in range(S5_W // 128):
            y_s[hf, rows, :] = y[:, hf * 128:(hf + 1) * 128]
    for r in range(8):
        out_ref[0, r * nseg:(r + 1) * nseg, :] = jnp.concatenate(
            [y_s[hf, pl.ds(r, nseg, stride=8), :] for hf in range(S5_W // 128)], axis=-1).astype(BF16)


def _s5(u, bm, cm, pw, d, gw, gb, og):
    B, _, L, _ = u.shape
    tl = min(S5_TL, L)
    grid = (B, L // tl)
    tspec = pl.BlockSpec((1, tl, S5_W), lambda b, i: (b, i, 0))
    uspec = pl.BlockSpec((1, S5_W // 128, tl, 128), lambda b, i: (b, 0, i, 0))
    return pl.pallas_call(
        functools.partial(_s5_kernel, tl=tl),
        grid=grid,
        in_specs=[uspec, _const_spec((S5_W, 2 * S5_LANES)), _const_spec((2 * S5_LANES, S5_W)),
                  _const_spec((10, 8, S5_LANES)), _const_spec((1, S5_W)), _const_spec((S5_W, S5_W)),
                  _const_spec((1, S5_W)), _const_spec((1, S5_W))],
        out_specs=tspec,
        out_shape=jax.ShapeDtypeStruct((B, L, S5_W), BF16),
        scratch_shapes=[pltpu.VMEM((tl, 2 * S5_LANES), F32), pltpu.VMEM((S5_W // 128, tl, 128), F32),
                        pltpu.VMEM((8, 2 * S5_LANES), F32)],
        compiler_params=pltpu.CompilerParams(
            dimension_semantics=("parallel", "arbitrary"), vmem_limit_bytes=VMEM_LIMIT),
        name="s5",
    )(u, bm, cm, pw, d, gw, gb, og)


def _ffn_kernel(x_ref, hm_ref, hs_ref, h5_ref, wom_ref, wos_ref, wo5_ref, g_ref,
                wg_ref, wv_ref, cw_ref, wd_ref, out_ref, h_s, gext_s, act_s, carry_s, *, tm):
    l = pl.program_id(1)

    @pl.when(l == 0)
    def _():
        carry_s[...] = jnp.zeros_like(carry_s)

    x1 = (x_ref[0] + _dot(hm_ref[0], wom_ref[...]) + _dot(hs_ref[0], wos_ref[...])
          + _dot(h5_ref[0], wo5_ref[...]))
    out_ref[0] = x1
    ms = jnp.mean(x1 * x1, axis=-1, keepdims=True)
    h_s[...] = (x1 * lax.rsqrt(ms + EPS) * g_ref[...]).astype(BF16)

    for c in range(D_FF // FFN_FC):
        cols = slice(c * FFN_FC, (c + 1) * FFN_FC)
        g = _dot(h_s[...], wg_ref[:, cols])
        v = _dot(h_s[...], wv_ref[:, cols])
        gext_s[0:8, :] = carry_s[c]
        gext_s[8:8 + tm, :] = g
        carry_s[c] = g[tm - 8:tm, :]
        conv = (cw_ref[2:3, cols] * g + cw_ref[1:2, cols] * gext_s[7:7 + tm, :]
                + cw_ref[0:1, cols] * gext_s[6:6 + tm, :])
        act_s[:, cols] = (conv * _sigmoid(conv) * v).astype(BF16)

    out_ref[0] += _dot(act_s[...], wd_ref[...])


def _ffn(x, hm, hs, h5, wom, wos, wo5, g, wg, wv, cw, wd):
    B, L, _ = x.shape
    tm = min(FFN_TM, L)
    grid = (B, L // tm)
    tspec = lambda w_: pl.BlockSpec((1, tm, w_), lambda b, i: (b, i, 0))
    return pl.pallas_call(
        functools.partial(_ffn_kernel, tm=tm),
        grid=grid,
        in_specs=[tspec(D_MODEL), tspec(ML_WP), tspec(SB_W), tspec(S5_W),
                  _const_spec((ML_WP, D_MODEL)), _const_spec((SB_W, D_MODEL)), _const_spec((S5_W, D_MODEL)),
                  _const_spec((1, D_MODEL)), _const_spec((D_MODEL, D_FF)), _const_spec((D_MODEL, D_FF)),
                  _const_spec((3, D_FF)), _const_spec((D_FF, D_MODEL))],
        out_specs=tspec(D_MODEL),
        out_shape=jax.ShapeDtypeStruct((B, L, D_MODEL), F32),
        scratch_shapes=[pltpu.VMEM((tm, D_MODEL), BF16), pltpu.VMEM((tm + 8, FFN_FC), F32),
                        pltpu.VMEM((tm, D_FF), BF16), pltpu.VMEM((D_FF // FFN_FC, 8, FFN_FC), F32)],
        compiler_params=pltpu.CompilerParams(
            dimension_semantics=("parallel", "arbitrary"), vmem_limit_bytes=VMEM_LIMIT),
        name="ffn",
    )(x, hm, hs, h5, wom, wos, wo5, g, wg, wv, cw, wd)


def _pad_heads(w):
    lead = w.shape[:-1]
    w = w.reshape(lead + (ML_HEADS, ML_DH))
    w = jnp.pad(w, [(0, 0)] * len(lead) + [(0, 0), (0, ML_DHP - ML_DH)])
    return w.reshape(lead + (ML_WP,))


def _s5_params(a_re, a_im, log_dt, b_re, b_im, c_re, c_im, nseg):
    lr = jnp.minimum(a_re, -1e-4)
    li = a_im
    dt = jnp.exp(log_dt)[:, None]
    mag = jnp.exp(lr * dt)
    br = mag * jnp.cos(li * dt)
    bi = mag * jnp.sin(li * dt)
    nr, ni = br - 1.0, bi
    den = lr * lr + li * li
    fr = (nr * lr + ni * li) / den
    fi = (ni * lr - nr * li) / den
    bbr = fr[..., None] * b_re - fi[..., None] * b_im
    bbi = fr[..., None] * b_im + fi[..., None] * b_re
    eye = jnp.eye(S5_GROUPS, dtype=F32)
    bm_re = jnp.einsum("gk,gph->ghkp", eye, bbr).reshape(S5_W, S5_LANES)
    bm_im = jnp.einsum("gk,gph->ghkp", eye, bbi).reshape(S5_W, S5_LANES)
    bm = jnp.concatenate([bm_re, bm_im], axis=1).astype(BF16)
    cm_re = jnp.einsum("gk,ghp->gpkh", eye, c_re).reshape(S5_LANES, S5_W)
    cm_im = jnp.einsum("gk,ghp->gpkh", eye, c_im).reshape(S5_LANES, S5_W)
    cm = jnp.concatenate([cm_re, -cm_im], axis=0).astype(BF16)
    def cpow2(r, i, n):
        for _ in range(n):
            r, i = r * r - i * i, 2.0 * r * i
        return r, i

    lr_, li_ = br.reshape(-1), bi.reshape(-1)
    assert nseg & (nseg - 1) == 0
    ar, ai = cpow2(lr_, li_, nseg.bit_length() - 1)
    pr, pi_ = [ar], [ai]
    for _ in range(7):
        r, i = pr[-1], pi_[-1]
        pr.append(r * ar - i * ai)
        pi_.append(r * ai + i * ar)
    row = jnp.arange(8)[:, None]
    planes = [jnp.broadcast_to(lr_[None, :], (8, S5_LANES)), jnp.broadcast_to(li_[None, :], (8, S5_LANES))]
    for k in (1, 2, 4):
        keep = row >= k
        planes.append(jnp.where(keep, pr[k - 1][None, :], 0.0))
        planes.append(jnp.where(keep, pi_[k - 1][None, :], 0.0))
    planes.append(jnp.stack(pr, axis=0))
    planes.append(jnp.stack(pi_, axis=0))
    pw = jnp.stack(planes, axis=0).astype(F32)
    return bm, cm, pw


def kernel(x, norm_mix_g, w_in, ml_conv_w, ml_conv_b, ml_gate_b, ml_out_g, sb_q_g, sb_k_g, sb_out_g, s5_a_re, s5_a_im, s5_log_dt, s5_b_re, s5_b_im, s5_c_re, s5_c_im, s5_d, s5_glu_w, s5_glu_b, s5_out_g, w_out, norm_ffn_g, ffn_w_up, ffn_conv_w, ffn_w_down):
    depth = w_in.shape[0]
    seg = jnp.kron(jnp.eye(SB_HEADS, dtype=F32), jnp.ones((SB_DH, SB_DH), F32)).astype(BF16)
    for l in range(depth):
        w = w_in[l]
        o0 = 0
        parts = []
        for wd_ in (ML_W, ML_W, ML_W, ML_W, 2 * ML_HEADS, SB_W, SB_W, SB_W, S5_W):
            parts.append(w[:, o0:o0 + wd_])
            o0 += wd_
        wq, wk, wv_, wo_, wif, wsq, wsk, wsv, wu = parts
        wfi = jnp.concatenate([wif[:, ML_HEADS:], wif[:, :ML_HEADS]], axis=1)
        wv_pad = _pad_heads(wv_)
        wv_pad = wv_pad.at[:, GATE_LANE:GATE_LANE + 8].set(wif)
        wv_pad = wv_pad.at[:, ML_DHP + GATE_LANE:ML_DHP + GATE_LANE + 8].set(wfi)
        w_all = jnp.concatenate(
            [_pad_heads(wq), _pad_heads(wk), wv_pad, _pad_heads(wo_), wsq, wsk, wsv, wu], axis=1).astype(BF16)
        qg = jnp.tile(sb_q_g[l], SB_HEADS)[None, :] * (SB_DH ** -0.5 * math.log2(math.e))
        kg = jnp.tile(sb_k_g[l], SB_HEADS)[None, :]
        (qk, mv, mo, sq, sk, sv, u, gr1, gr2, gc1, gc2) = _in_proj(
            x, norm_mix_g[l][None, :], w_all, seg, qg, kg)

        cw = jnp.concatenate([_pad_heads(ml_conv_w[l][:, :ML_W]), _pad_heads(ml_conv_w[l][:, ML_W:])], axis=1)
        cb = jnp.concatenate([_pad_heads(ml_conv_b[l][:ML_W]), _pad_heads(ml_conv_b[l][ML_W:])])[None, :]
        gb = ml_gate_b[l]
        gb2 = jnp.concatenate([gb[ML_HEADS:], gb[:ML_HEADS]])
        hm = _mlstm(qk, mv, mo, gr1, gr2, gc1, gc2, cw, cb, gb[:, None], gb2[:, None], gb[None, :],
                    gb2[None, :], _pad_heads(ml_out_g[l].reshape(-1))[None, :])

        stop = (1.05 * SB_DH * jnp.max(jnp.abs(qg)) * jnp.max(jnp.abs(kg)) + 160.0).reshape(1).astype(F32)
        hs = _sb_attn(stop, sq, sk, sv, sb_out_g[l].reshape(1, SB_W))

        bm, cm, pw = _s5_params(s5_a_re[l], s5_a_im[l], s5_log_dt[l], s5_b_re[l], s5_b_im[l],
                                s5_c_re[l], s5_c_im[l], min(S5_TL, x.shape[1]) // 8)
        h5 = _s5(u, bm, cm, pw, s5_d[l].reshape(1, S5_W), s5_glu_w[l].astype(BF16),
                 s5_glu_b[l][None, :], s5_out_g[l][None, :])

        wo = w_out[l]
        wom = jnp.pad(wo[:ML_W].reshape(ML_HEADS, ML_DH, D_MODEL),
                      [(0, 0), (0, ML_DHP - ML_DH), (0, 0)]).reshape(ML_WP, D_MODEL).astype(BF16)
        wos = wo[ML_W:ML_W + SB_W].astype(BF16)
        wo5 = wo[ML_W + SB_W:].astype(BF16)
        x = _ffn(x, hm, hs, h5, wom, wos, wo5, norm_ffn_g[l][None, :],
                 ffn_w_up[l][:, :D_FF].astype(BF16), ffn_w_up[l][:, D_FF:].astype(BF16),
                 ffn_conv_w[l], ffn_w_down[l].astype(BF16))
    return x
```

```python
import functools
import math

import jax
import jax.numpy as jnp
from jax import lax
from jax.experimental import pallas as pl
from jax.experimental.pallas import tpu as pltpu

F32 = jnp.float32
BF16 = jnp.bfloat16

D_MODEL = 1024
ML_HEADS = 4
ML_DH = 96
ML_DHP = 128
ML_WP = ML_HEADS * ML_DHP
ML_W = ML_HEADS * ML_DH
ML_CONV = 4
SB_HEADS = 6
SB_DH = 64
SB_W = SB_HEADS * SB_DH
S5_W = 256
S5_GROUPS = 16
S5_CH = 16
S5_STATE = 64
S5_LANES = S5_GROUPS * S5_STATE
D_FF = 2816
EPS = 1e-6
NEG = -1e30

C_QK, C_V, C_O, C_SQ, C_SK, C_SV, C_U, C_END = (0, 1024, 1536, 2048, 2432, 2816, 3200, 3456)
GATE_LANE = 104

VMEM_LIMIT = 56 * 1024 * 1024

IN_TM = 1024
ML_TL = 1024
ML_C = 128
SB_TQ = 256
SB_TK = 256
S5_TL = 1024
FFN_TM = 512
FFN_FC = 256


def _const_spec(shape):
    nd = len(shape)
    return pl.BlockSpec(shape, lambda *_: (0,) * nd, pipeline_mode=pl.Buffered(1))


def _dot(a, b):
    return jnp.dot(a, b, preferred_element_type=F32)


def _dot_nt(a, b):
    return lax.dot_general(a, b, (((1,), (1,)), ((), ())), preferred_element_type=F32)


def _dot_tn(a, b):
    return lax.dot_general(a, b, (((0,), (0,)), ((), ())), preferred_element_type=F32)


def _split(x):
    hi = x.astype(BF16)
    lo = (x - hi.astype(F32)).astype(BF16)
    return hi, lo


def _log_sigmoid(x):
    return jnp.minimum(x, 0.0) - jnp.log(1.0 + jnp.exp(-jnp.abs(x)))


def _sigmoid(x):
    return 1.0 / (1.0 + jnp.exp(-x))


def _in_proj_kernel(x_ref, g_ref, w_ref, seg_ref, qg_ref, kg_ref,
                    qk_ref, v_ref, o_ref, sq_ref, sk_ref, sv_ref, u_ref,
                    gr1_ref, gr2_ref, gc1_ref, gc2_ref):
    x = x_ref[0]
    ms = jnp.mean(x * x, axis=-1, keepdims=True)
    h = (x * lax.rsqrt(ms + EPS) * g_ref[...]).astype(BF16)

    def proj(a, b):
        return _dot(h, w_ref[:, a:b])

    def qk_norm(z, gain):
        ss = _dot((z * z).astype(BF16), seg_ref[...])
        return (z * lax.rsqrt(ss * (1.0 / SB_DH) + EPS) * gain).astype(BF16)

    qk_ref[0] = proj(C_QK, C_V)
    zv = proj(C_V, C_O)
    v_ref[0] = zv.astype(BF16)
    o_ref[0] = proj(C_O, C_SQ).astype(BF16)
    sq_ref[0] = qk_norm(proj(C_SQ, C_SK), qg_ref[...])
    sk_ref[0] = qk_norm(proj(C_SK, C_SV), kg_ref[...])
    sv_ref[0] = proj(C_SV, C_U).astype(BF16)
    zu = proj(C_U, C_END)
    nseg = zu.shape[0] // 8
    for r in range(8):
        for hf in range(S5_W // 128):
            u_ref[0, hf, pl.ds(r, nseg, stride=8), :] = zu[r * nseg:(r + 1) * nseg, hf * 128:(hf + 1) * 128]
    gates = slice(GATE_LANE, GATE_LANE + 8)
    z1 = zv[:, :ML_DHP]
    z2 = zv[:, ML_DHP:2 * ML_DHP]
    gc1_ref[0] = z1[:, gates]
    gc2_ref[0] = z2[:, gates]
    gr1_ref[0] = z1.T[gates, :]
    gr2_ref[0] = z2.T[gates, :]


def _in_proj(x, g, w, seg, qg, kg):
    B, L, _ = x.shape
    tm = min(IN_TM, L)
    grid = (B, L // tm)
    tok = lambda w_, dt: jax.ShapeDtypeStruct((B, L, w_), dt)
    tspec = lambda w_: pl.BlockSpec((1, tm, w_), lambda b, i: (b, i, 0))
    rspec = pl.BlockSpec((1, 8, tm), lambda b, i: (b, 0, i))
    return pl.pallas_call(
        _in_proj_kernel,
        grid=grid,
        in_specs=[tspec(D_MODEL), _const_spec((1, D_MODEL)), _const_spec((D_MODEL, C_END)),
                  _const_spec((SB_W, SB_W)), _const_spec((1, SB_W)), _const_spec((1, SB_W))],
        out_specs=[tspec(2 * ML_WP), tspec(ML_WP), tspec(ML_WP), tspec(SB_W), tspec(SB_W), tspec(SB_W),
                   pl.BlockSpec((1, S5_W // 128, tm, 128), lambda b, i: (b, 0, i, 0)),
                   rspec, rspec, tspec(8), tspec(8)],
        out_shape=[tok(2 * ML_WP, F32), tok(ML_WP, BF16), tok(ML_WP, BF16), tok(SB_W, BF16),
                   tok(SB_W, BF16), tok(SB_W, BF16), jax.ShapeDtypeStruct((B, S5_W // 128, L, 128), F32),
                   jax.ShapeDtypeStruct((B, 8, L), F32), jax.ShapeDtypeStruct((B, 8, L), F32),
                   tok(8, F32), tok(8, F32)],
        compiler_params=pltpu.CompilerParams(
            dimension_semantics=("parallel", "parallel"), vmem_limit_bytes=VMEM_LIMIT),
        name="in_proj",
    )(x, g, w, seg, qg, kg)


def _mlstm_kernel(qk_ref, v_ref, o_ref, gr1_ref, gr2_ref, gc1_ref, gc2_ref,
                  cw_ref, cb_ref, gbr1_ref, gbr2_ref, gbc1_ref, gbc2_ref, og_ref,
                  out_ref, ext_s, st_s, m_s, *, tl):
    l = pl.program_id(1)

    @pl.when(l == 0)
    def _():
        ext_s[0:8, :] = jnp.zeros((8, 2 * ML_WP), F32)
        st_s[...] = jnp.zeros_like(st_s)
        m_s[...] = jnp.zeros_like(m_s)

    pre = qk_ref[0]
    ext_s[8:8 + tl, :] = pre
    conv = (cw_ref[3:4, :] * pre + cw_ref[2:3, :] * ext_s[7:7 + tl, :]
            + cw_ref[1:2, :] * ext_s[6:6 + tl, :] + cw_ref[0:1, :] * ext_s[5:5 + tl, :] + cb_ref[...])
    ext_s[0:8, :] = pre[tl - 8:tl, :]
    qk = conv * _sigmoid(conv)
    q_all = qk[:, :ML_WP].astype(BF16)
    k_all = qk[:, ML_WP:] * (ML_DH ** -0.5)

    gr1 = gr1_ref[0] + gbr1_ref[...]
    gr2 = gr2_ref[0] + gbr2_ref[...]
    gc1 = gc1_ref[0] + gbc1_ref[...]
    gc2 = gc2_ref[0] + gbc2_ref[...]

    ri = lax.broadcasted_iota(jnp.int32, (ML_C, ML_C), 0)
    ci = lax.broadcasted_iota(jnp.int32, (ML_C, ML_C), 1)
    visible = ri <= ci
    tri_low = jnp.where(ci <= ri, 1.0, 0.0).astype(BF16)
    tri_up = jnp.where(visible, 1.0, 0.0).astype(BF16)
    lane = lax.broadcasted_iota(jnp.int32, (ML_C, ML_DHP), 1)

    for c in range(tl // ML_C):
        rows = slice(c * ML_C, (c + 1) * ML_C)
        lf_hi, lf_lo = _split(_log_sigmoid(gc2[rows, :]))
        r_col = gc1[rows, :] - (_dot(tri_low, lf_hi) + _dot(tri_low, lf_lo))
        ir = gr1[:, rows]
        lr_hi, lr_lo = _split(_log_sigmoid(gr2[:, rows]))
        b_row = _dot(lr_hi, tri_up) + _dot(lr_lo, tri_up)
        btot = b_row[:, ML_C - 1:ML_C]
        mp = m_s[...]
        wlog = btot - b_row + ir
        m_loc = jnp.max(wlog, axis=-1, keepdims=True)
        wexp = jnp.exp(wlog - m_loc)
        m_new = jnp.maximum(btot + mp, m_loc)
        a_sc = jnp.exp(btot + mp - m_new)
        c_sc = jnp.exp(m_loc - m_new)
        inter_log = b_row + mp

        for hd in range(ML_HEADS):
            cols = slice(hd * ML_DHP, (hd + 1) * ML_DHP)
            head = slice(hd, hd + 1)
            qh = q_all[rows, cols]
            kh = k_all[rows, cols].astype(BF16)
            dlog = jnp.where(visible, r_col[:, head] + b_row[head, :], NEG)
            m_t = jnp.maximum(inter_log[head, :], jnp.max(dlog, axis=0, keepdims=True))
            p_t = (_dot_nt(kh, qh) * jnp.exp(dlog - m_t)).astype(BF16)
            v_t = jnp.where(lane == ML_DH, 1.0, v_ref[0, rows, cols].astype(F32)).T
            st = st_s[hd]
            isc = jnp.exp(inter_log[head, :] - m_t)
            acc = _dot(v_t.astype(BF16), p_t) + isc * _dot_nt(st.astype(BF16), qh)
            den = acc[ML_DH:ML_DH + 1, :]
            hv = acc[:ML_DH, :] * (1.0 / jnp.maximum(jnp.abs(den), jnp.exp(-m_t)))
            ss = jnp.sum(hv * hv, axis=0, keepdims=True) * (1.0 / ML_DH)
            hn_t = jnp.concatenate([hv * lax.rsqrt(ss + EPS), jnp.zeros((ML_DHP - ML_DH, ML_C), F32)], axis=0)
            hn = hn_t.T * og_ref[:, cols] * _sigmoid(o_ref[0, rows, cols].astype(F32))
            out_ref[0, rows, cols] = hn.astype(BF16)
            vw = (v_t * wexp[head, :]).astype(BF16)
            st_s[hd] = a_sc[head, :] * st + c_sc[head, :] * _dot(vw, kh)
        m_s[...] = m_new


def _mlstm(qk, v, o, gr1, gr2, gc1, gc2, cw, cb, gbr1, gbr2, gbc1, gbc2, og):
    B, L, _ = qk.shape
    tl = min(ML_TL, L)
    grid = (B, L // tl)
    tspec = lambda w_: pl.BlockSpec((1, tl, w_), lambda b, i: (b, i, 0))
    rspec = pl.BlockSpec((1, 8, tl), lambda b, i: (b, 0, i))
    return pl.pallas_call(
        functools.partial(_mlstm_kernel, tl=tl),
        grid=grid,
        in_specs=[tspec(2 * ML_WP), tspec(ML_WP), tspec(ML_WP), rspec, rspec, tspec(8), tspec(8),
                  _const_spec((ML_CONV, 2 * ML_WP)), _const_spec((1, 2 * ML_WP)),
                  _const_spec((8, 1)), _const_spec((8, 1)), _const_spec((1, 8)), _const_spec((1, 8)),
                  _const_spec((1, ML_WP))],
        out_specs=tspec(ML_WP),
        out_shape=jax.ShapeDtypeStruct((B, L, ML_WP), BF16),
        scratch_shapes=[pltpu.VMEM((tl + 8, 2 * ML_WP), F32),
                        pltpu.VMEM((ML_HEADS, ML_DHP, ML_DHP), F32),
                        pltpu.VMEM((8, 1), F32)],
        compiler_params=pltpu.CompilerParams(
            dimension_semantics=("parallel", "arbitrary"), vmem_limit_bytes=VMEM_LIMIT),
        name="mlstm",
    )(qk, v, o, gr1, gr2, gc1, gc2, cw, cb, gbr1, gbr2, gbc1, gbc2, og)


def _sb_kernel(stop_ref, q_ref, k_ref, v_ref, og_ref, out_ref, acc_s, r_s):
    i = pl.program_id(1)
    npair = SB_HEADS // 2
    lane = lax.broadcasted_iota(jnp.int32, (SB_TQ, 2 * SB_DH), 1)
    first = lane < SB_DH
    q_heads = []
    for p in range(npair):
        q = q_ref[0, :, p * 128:(p + 1) * 128]
        zero = jnp.zeros_like(q)
        q_heads.append(jnp.where(first, q, zero))
        q_heads.append(jnp.where(first, zero, q))
    ri = lax.broadcasted_iota(jnp.int32, (SB_TQ, SB_TK), 0)
    ci = lax.broadcasted_iota(jnp.int32, (SB_TQ, SB_TK), 1)
    rj = lax.broadcasted_iota(jnp.int32, (SB_TK, SB_TK), 0)
    cj = lax.broadcasted_iota(jnp.int32, (SB_TK, SB_TK), 1)
    suffix = jnp.where(rj >= cj, 1.0, 0.0).astype(BF16)
    sign = jnp.uint32(0x80000000)

    acc_s[...] = jnp.zeros_like(acc_s)
    r_s[...] = jnp.zeros_like(r_s)

    def blocks(tiles):
        units = [(t, hd) for t in range(len(tiles)) for hd in range(SB_HEADS)]
        ks, vs, zs, sbs = {}, {}, {}, {}
        for t, (j, _) in enumerate(tiles):
            start = pl.multiple_of(j * SB_TK, SB_TK)
            for p in range(npair):
                ks[t, p] = k_ref[0, pl.ds(start, SB_TK), p * 128:(p + 1) * 128]
                vs[t, p] = v_ref[0, pl.ds(start, SB_TK), p * 128:(p + 1) * 128]

        def scores(u):
            t, hd = u
            zs[u] = _dot_nt(q_heads[hd], ks[t, hd // 2])

        def suffix_sums(u):
            causal = tiles[u[0]][1]
            z = zs[u]
            neg_abs = lax.bitcast_convert_type(lax.bitcast_convert_type(z, jnp.uint32) | sign, F32)
            sp = jnp.maximum(z, 0.0) + jnp.log2(1.0 + jnp.exp2(neg_abs))
            if causal is not None:
                sp = jnp.where(causal, sp, 0.0)
            sbs[u] = _dot(sp.astype(BF16), suffix)

        def weights_and_values(u):
            t, hd = u
            causal = tiles[t][1]
            r = r_s[hd]
            a = jnp.exp2(zs[u] - (sbs[u] + r))
            if causal is not None:
                a = jnp.where(causal, a, 0.0)
            r_s[hd] = r + sbs[u][:, 0:1]
            acc_s[hd] += _dot(a.astype(BF16), vs[t, hd // 2])

        stages = (scores, suffix_sums, weights_and_values)
        for step in range(len(units) + len(stages) - 1):
            for lag, stage in enumerate(stages):
                if 0 <= step - lag < len(units):
                    stage(units[step - lag])

    nd = SB_TQ // SB_TK
    blocks([(i * nd + d, ci + d * SB_TK < ri) for d in reversed(range(nd))])

    def min_suffix():
        m = r_s[0]
        for hd in range(1, SB_HEADS):
            m = jnp.minimum(m, r_s[hd])
        return jnp.min(m)

    n_left = i * nd

    def cond(carry):
        t, rmin = carry
        return jnp.logical_and(t < n_left, rmin <= stop_ref[0])

    def body(carry):
        t, _ = carry
        blocks([(n_left - 1 - t, None)])
        return t + 1, min_suffix()

    lax.while_loop(cond, body, (jnp.int32(0), min_suffix()))

    for p in range(npair):
        o = jnp.where(first, acc_s[2 * p], acc_s[2 * p + 1])
        o2 = o * o
        ss_a = jnp.sum(jnp.where(first, o2, 0.0), axis=-1, keepdims=True)
        ss_b = jnp.sum(jnp.where(first, 0.0, o2), axis=-1, keepdims=True)
        ss = jnp.where(first, ss_a, ss_b) * (1.0 / SB_DH)
        out_ref[0, :, p * 128:(p + 1) * 128] = (
            o * lax.rsqrt(ss + EPS) * og_ref[:, p * 128:(p + 1) * 128]).astype(BF16)


def _sb_attn(stop, q, k, v, og):
    B, L, _ = q.shape
    assert L % SB_TQ == 0 and SB_TQ % SB_TK == 0
    grid = (B, L // SB_TQ)
    return pl.pallas_call(
        _sb_kernel,
        grid=grid,
        in_specs=[pl.BlockSpec(memory_space=pltpu.SMEM),
                  pl.BlockSpec((1, SB_TQ, SB_W), lambda b, i: (b, i, 0)),
                  pl.BlockSpec((1, L, SB_W), lambda b, i: (b, 0, 0)),
                  pl.BlockSpec((1, L, SB_W), lambda b, i: (b, 0, 0)),
                  _const_spec((1, SB_W))],
        out_specs=pl.BlockSpec((1, SB_TQ, SB_W), lambda b, i: (b, i, 0)),
        out_shape=jax.ShapeDtypeStruct((B, L, SB_W), BF16),
        scratch_shapes=[pltpu.VMEM((SB_HEADS, SB_TQ, 2 * SB_DH), F32),
                        pltpu.VMEM((SB_HEADS, SB_TQ, 1), F32)],
        compiler_params=pltpu.CompilerParams(
            dimension_semantics=("parallel", "parallel"), vmem_limit_bytes=VMEM_LIMIT),
        name="sb_attn",
    )(stop, q, k, v, og)


def _gelu_tanh(x):
    return 0.5 * x * (1.0 + jnp.tanh(math.sqrt(2.0 / math.pi) * (x + 0.044715 * (x * x * x))))


def _cmul(ar, ai, br, bi):
    return ar * br - ai * bi, ar * bi + ai * br


def _s5_kernel(u_ref, bm_ref, cm_ref, pw_ref, d_ref, gw_ref, gb_ref, og_ref, out_ref, xs_s, y_s, carry_s, *, tl):
    l = pl.program_id(1)
    nseg = tl // 8
    nlt = S5_LANES // 128
    grp = 8

    @pl.when(l == 0)
    def _():
        carry_s[...] = jnp.zeros_like(carry_s)

    u = jnp.concatenate([u_ref[0, hf] for hf in range(S5_W // 128)], axis=-1)
    xs_s[...] = _dot(u.astype(BF16), bm_ref[...])

    def lanes(lt):
        return slice(lt * 128, (lt + 1) * 128), slice(S5_LANES + lt * 128, S5_LANES + (lt + 1) * 128)

    for g0 in range(0, nlt, grp):
        tiles = range(g0, g0 + grp)
        lam = [(pw_ref[0, :, lanes(lt)[0]], pw_ref[1, :, lanes(lt)[0]]) for lt in tiles]

        def local_step(j, st):
            r0 = pl.multiple_of(j * 8, 8)
            new = []
            for n, lt in enumerate(tiles):
                re_l, im_l = lanes(lt)
                pr, pi_ = _cmul(lam[n][0], lam[n][1], st[2 * n], st[2 * n + 1])
                xr = xs_s[pl.ds(r0, 8), re_l] + pr
                xi = xs_s[pl.ds(r0, 8), im_l] + pi_
                xs_s[pl.ds(r0, 8), re_l] = xr
                xs_s[pl.ds(r0, 8), im_l] = xi
                new += [xr, xi]
            return tuple(new)

        fin = lax.fori_loop(0, nseg, local_step, tuple(jnp.zeros((8, 128), F32) for _ in range(2 * grp)),
                            unroll=4)

        start = []
        for n, lt in enumerate(tiles):
            re_l, im_l = lanes(lt)
            er, ei = fin[2 * n], fin[2 * n + 1]
            for m, k in enumerate((1, 2, 4)):
                pr, pi_ = _cmul(pw_ref[2 + 2 * m, :, re_l], pw_ref[3 + 2 * m, :, re_l],
                                pltpu.roll(er, k, 0), pltpu.roll(ei, k, 0))
                er, ei = er + pr, ei + pi_
            s0r, s0i = carry_s[:, re_l], carry_s[:, im_l]
            pr, pi_ = _cmul(pw_ref[8, :, re_l], pw_ref[9, :, re_l], s0r, s0i)
            er, ei = er + pr, ei + pi_
            carry_s[:, re_l] = jnp.broadcast_to(er[7:8, :], (8, 128))
            carry_s[:, im_l] = jnp.broadcast_to(ei[7:8, :], (8, 128))
            row = lax.broadcasted_iota(jnp.int32, (8, 128), 0)
            start += [jnp.where(row == 0, s0r, pltpu.roll(er, 1, 0)),
                      jnp.where(row == 0, s0i, pltpu.roll(ei, 1, 0))]

        def fix_step(j, tr):
            r0 = pl.multiple_of(j * 8, 8)
            new = []
            for n, lt in enumerate(tiles):
                re_l, im_l = lanes(lt)
                pr, pi_ = _cmul(lam[n][0], lam[n][1], tr[2 * n], tr[2 * n + 1])
                xs_s[pl.ds(r0, 8), re_l] += pr
                xs_s[pl.ds(r0, 8), im_l] += pi_
                new += [pr, pi_]
            return tuple(new)

        lax.fori_loop(0, nseg, fix_step, tuple(start), unroll=4)

    rb = min(256, tl)
    for r0 in range(0, tl, rb):
        rows = slice(r0, r0 + rb)
        y = _dot(xs_s[rows, :].astype(BF16), cm_ref[...]) + d_ref[...] * u[rows, :]
        y = _gelu_tanh(y)
        y = y * _sigmoid(_dot(y.astype(BF16), gw_ref[...]) + gb_ref[...])
        ms = jnp.mean(y * y, axis=-1, keepdims=True)
        y = y * lax.rsqrt(ms + EPS) * og_ref[...]
        for hf in range(S5_W // 128):
            y_s[hf, rows, :] = y[:, hf * 128:(hf + 1) * 128]
    for r in range(8):
        out_ref[0, r * nseg:(r + 1) * nseg, :] = jnp.concatenate(
            [y_s[hf, pl.ds(r, nseg, stride=8), :] for hf in range(S5_W // 128)], axis=-1).astype(BF16)


def _s5(u, bm, cm, pw, d, gw, gb, og):
    B, _, L, _ = u.shape
    tl = min(S5_TL, L)
    grid = (B, L // tl)
    tspec = pl.BlockSpec((1, tl, S5_W), lambda b, i: (b, i, 0))
    uspec = pl.BlockSpec((1, S5_W // 128, tl, 128), lambda b, i: (b, 0, i, 0))
    return pl.pallas_call(
        functools.partial(_s5_kernel, tl=tl),
        grid=grid,
        in_specs=[uspec, _const_spec((S5_W, 2 * S5_LANES)), _const_spec((2 * S5_LANES, S5_W)),
                  _const_spec((10, 8, S5_LANES)), _const_spec((1, S5_W)), _const_spec((S5_W, S5_W)),
                  _const_spec((1, S5_W)), _const_spec((1, S5_W))],
        out_specs=tspec,
        out_shape=jax.ShapeDtypeStruct((B, L, S5_W), BF16),
        scratch_shapes=[pltpu.VMEM((tl, 2 * S5_LANES), F32), pltpu.VMEM((S5_W // 128, tl, 128), F32),
                        pltpu.VMEM((8, 2 * S5_LANES), F32)],
        compiler_params=pltpu.CompilerParams(
            dimension_semantics=("parallel", "arbitrary"), vmem_limit_bytes=VMEM_LIMIT),
        name="s5",
    )(u, bm, cm, pw, d, gw, gb, og)


def _ffn_kernel(x_ref, hm_ref, hs_ref, h5_ref, wom_ref, wos_ref, wo5_ref, g_ref,
                wu_ref, cw_ref, wd_ref, out_ref, h_s, gext_s, act_s, carry_s, *, tm):
    l = pl.program_id(1)

    @pl.when(l == 0)
    def _():
        carry_s[...] = jnp.zeros_like(carry_s)

    x1 = (x_ref[0] + _dot(hm_ref[0], wom_ref[...]) + _dot(hs_ref[0], wos_ref[...])
          + _dot(h5_ref[0], wo5_ref[...]))
    out_ref[0] = x1
    ms = jnp.mean(x1 * x1, axis=-1, keepdims=True)
    h_s[...] = (x1 * lax.rsqrt(ms + EPS) * g_ref[...]).astype(BF16)

    for c in range(D_FF // FFN_FC):
        cols = slice(c * FFN_FC, (c + 1) * FFN_FC)
        g = _dot(h_s[...], wu_ref[:, cols])
        v = _dot(h_s[...], wu_ref[:, D_FF + c * FFN_FC:D_FF + (c + 1) * FFN_FC])
        gext_s[0:8, :] = carry_s[c]
        gext_s[8:8 + tm, :] = g
        carry_s[c] = g[tm - 8:tm, :]
        conv = (cw_ref[2:3, cols] * g + cw_ref[1:2, cols] * gext_s[7:7 + tm, :]
                + cw_ref[0:1, cols] * gext_s[6:6 + tm, :])
        act_s[:, cols] = (conv * _sigmoid(conv) * v).astype(BF16)

    out_ref[0] += _dot(act_s[...], wd_ref[...])


def _ffn(x, hm, hs, h5, wom, wos, wo5, g, wu, cw, wd):
    B, L, _ = x.shape
    tm = min(FFN_TM, L)
    grid = (B, L // tm)
    tspec = lambda w_: pl.BlockSpec((1, tm, w_), lambda b, i: (b, i, 0))
    return pl.pallas_call(
        functools.partial(_ffn_kernel, tm=tm),
        grid=grid,
        in_specs=[tspec(D_MODEL), tspec(ML_WP), tspec(SB_W), tspec(S5_W),
                  _const_spec((ML_WP, D_MODEL)), _const_spec((SB_W, D_MODEL)), _const_spec((S5_W, D_MODEL)),
                  _const_spec((1, D_MODEL)), _const_spec((D_MODEL, 2 * D_FF)),
                  _const_spec((3, D_FF)), _const_spec((D_FF, D_MODEL))],
        out_specs=tspec(D_MODEL),
        out_shape=jax.ShapeDtypeStruct((B, L, D_MODEL), F32),
        scratch_shapes=[pltpu.VMEM((tm, D_MODEL), BF16), pltpu.VMEM((tm + 8, FFN_FC), F32),
                        pltpu.VMEM((tm, D_FF), BF16), pltpu.VMEM((D_FF // FFN_FC, 8, FFN_FC), F32)],
        compiler_params=pltpu.CompilerParams(
            dimension_semantics=("parallel", "arbitrary"), vmem_limit_bytes=VMEM_LIMIT),
        name="ffn",
    )(x, hm, hs, h5, wom, wos, wo5, g, wu, cw, wd)


def _pad_heads(w):
    lead = w.shape[:-1]
    w = w.reshape(lead + (ML_HEADS, ML_DH))
    w = jnp.pad(w, [(0, 0)] * len(lead) + [(0, 0), (0, ML_DHP - ML_DH)])
    return w.reshape(lead + (ML_WP,))


def _s5_params(a_re, a_im, log_dt, b_re, b_im, c_re, c_im, nseg):
    lr = jnp.minimum(a_re, -1e-4)
    li = a_im
    dt = jnp.exp(log_dt)[:, None]
    mag = jnp.exp(lr * dt)
    br = mag * jnp.cos(li * dt)
    bi = mag * jnp.sin(li * dt)
    nr, ni = br - 1.0, bi
    den = lr * lr + li * li
    fr = (nr * lr + ni * li) / den
    fi = (ni * lr - nr * li) / den
    bbr = fr[..., None] * b_re - fi[..., None] * b_im
    bbi = fr[..., None] * b_im + fi[..., None] * b_re
    eye = jnp.eye(S5_GROUPS, dtype=F32)
    bm_re = jnp.einsum("gk,gph->ghkp", eye, bbr).reshape(S5_W, S5_LANES)
    bm_im = jnp.einsum("gk,gph->ghkp", eye, bbi).reshape(S5_W, S5_LANES)
    bm = jnp.concatenate([bm_re, bm_im], axis=1).astype(BF16)
    cm_re = jnp.einsum("gk,ghp->gpkh", eye, c_re).reshape(S5_LANES, S5_W)
    cm_im = jnp.einsum("gk,ghp->gpkh", eye, c_im).reshape(S5_LANES, S5_W)
    cm = jnp.concatenate([cm_re, -cm_im], axis=0).astype(BF16)
    def cpow2(r, i, n):
        for _ in range(n):
            r, i = r * r - i * i, 2.0 * r * i
        return r, i

    lr_, li_ = br.reshape(-1), bi.reshape(-1)
    assert nseg & (nseg - 1) == 0
    ar, ai = cpow2(lr_, li_, nseg.bit_length() - 1)
    pr, pi_ = [ar], [ai]
    for _ in range(7):
        r, i = pr[-1], pi_[-1]
        pr.append(r * ar - i * ai)
        pi_.append(r * ai + i * ar)
    row = jnp.arange(8)[:, None]
    planes = [jnp.broadcast_to(lr_[None, :], (8, S5_LANES)), jnp.broadcast_to(li_[None, :], (8, S5_LANES))]
    for k in (1, 2, 4):
        keep = row >= k
        planes.append(jnp.where(keep, pr[k - 1][None, :], 0.0))
        planes.append(jnp.where(keep, pi_[k - 1][None, :], 0.0))
    planes.append(jnp.stack(pr, axis=0))
    planes.append(jnp.stack(pi_, axis=0))
    pw = jnp.stack(planes, axis=0).astype(F32)
    return bm, cm, pw


def kernel(x, norm_mix_g, w_in, ml_conv_w, ml_conv_b, ml_gate_b, ml_out_g, sb_q_g, sb_k_g, sb_out_g, s5_a_re, s5_a_im, s5_log_dt, s5_b_re, s5_b_im, s5_c_re, s5_c_im, s5_d, s5_glu_w, s5_glu_b, s5_out_g, w_out, norm_ffn_g, ffn_w_up, ffn_conv_w, ffn_w_down):
    depth = w_in.shape[0]
    w_up_b = ffn_w_up.astype(BF16)
    w_down_b = ffn_w_down.astype(BF16)
    w_out_b = w_out.astype(BF16)
    bm_all, cm_all, pw_all = jax.vmap(functools.partial(_s5_params, nseg=min(S5_TL, x.shape[1]) // 8))(
        s5_a_re, s5_a_im, s5_log_dt, s5_b_re, s5_b_im, s5_c_re, s5_c_im)
    seg = jnp.kron(jnp.eye(SB_HEADS, dtype=F32), jnp.ones((SB_DH, SB_DH), F32)).astype(BF16)
    for l in range(depth):
        w = w_in[l]
        o0 = 0
        parts = []
        for wd_ in (ML_W, ML_W, ML_W, ML_W, 2 * ML_HEADS, SB_W, SB_W, SB_W, S5_W):
            parts.append(w[:, o0:o0 + wd_])
            o0 += wd_
        wq, wk, wv_, wo_, wif, wsq, wsk, wsv, wu = parts
        wfi = jnp.concatenate([wif[:, ML_HEADS:], wif[:, :ML_HEADS]], axis=1)
        wv_pad = _pad_heads(wv_)
        wv_pad = wv_pad.at[:, GATE_LANE:GATE_LANE + 8].set(wif)
        wv_pad = wv_pad.at[:, ML_DHP + GATE_LANE:ML_DHP + GATE_LANE + 8].set(wfi)
        w_all = jnp.concatenate(
            [_pad_heads(wq), _pad_heads(wk), wv_pad, _pad_heads(wo_), wsq, wsk, wsv, wu], axis=1).astype(BF16)
        qg = jnp.tile(sb_q_g[l], SB_HEADS)[None, :] * (SB_DH ** -0.5 * math.log2(math.e))
        kg = jnp.tile(sb_k_g[l], SB_HEADS)[None, :]
        (qk, mv, mo, sq, sk, sv, u, gr1, gr2, gc1, gc2) = _in_proj(
            x, norm_mix_g[l][None, :], w_all, seg, qg, kg)

        cw = jnp.concatenate([_pad_heads(ml_conv_w[l][:, :ML_W]), _pad_heads(ml_conv_w[l][:, ML_W:])], axis=1)
        cb = jnp.concatenate([_pad_heads(ml_conv_b[l][:ML_W]), _pad_heads(ml_conv_b[l][ML_W:])])[None, :]
        gb = ml_gate_b[l]
        gb2 = jnp.concatenate([gb[ML_HEADS:], gb[:ML_HEADS]])
        hm = _mlstm(qk, mv, mo, gr1, gr2, gc1, gc2, cw, cb, gb[:, None], gb2[:, None], gb[None, :],
                    gb2[None, :], _pad_heads(ml_out_g[l].reshape(-1))[None, :])

        stop = (1.05 * SB_DH * jnp.max(jnp.abs(qg)) * jnp.max(jnp.abs(kg)) + 160.0).reshape(1).astype(F32)
        hs = _sb_attn(stop, sq, sk, sv, sb_out_g[l].reshape(1, SB_W))

        h5 = _s5(u, bm_all[l], cm_all[l], pw_all[l], s5_d[l].reshape(1, S5_W), s5_glu_w[l].astype(BF16),
                 s5_glu_b[l][None, :], s5_out_g[l][None, :])

        wo = w_out_b[l]
        wom = jnp.pad(wo[:ML_W].reshape(ML_HEADS, ML_DH, D_MODEL),
                      [(0, 0), (0, ML_DHP - ML_DH), (0, 0)]).reshape(ML_WP, D_MODEL)
        x = _ffn(x, hm, hs, h5, wom, wo[ML_W:ML_W + SB_W], wo[ML_W + SB_W:], norm_ffn_g[l][None, :],
                 w_up_b[l], ffn_conv_w[l], w_down_b[l])
    return x
```

```python
import functools
import math

import jax
import jax.numpy as jnp
from jax import lax
from jax.experimental import pallas as pl
from jax.experimental.pallas import tpu as pltpu

F32 = jnp.float32
BF16 = jnp.bfloat16

D_MODEL = 1024
ML_HEADS = 4
ML_DH = 96
ML_DHP = 128
ML_WP = ML_HEADS * ML_DHP
ML_W = ML_HEADS * ML_DH
ML_CONV = 4
SB_HEADS = 6
SB_DH = 64
SB_W = SB_HEADS * SB_DH
S5_W = 256
S5_GROUPS = 16
S5_CH = 16
S5_STATE = 64
S5_LANES = S5_GROUPS * S5_STATE
D_FF = 2816
EPS = 1e-6
NEG = -1e30

C_QK, C_V, C_O, C_SQ, C_SK, C_SV, C_U, C_END = (0, 1024, 1536, 2048, 2432, 2816, 3200, 3456)
GATE_LANE = 104

VMEM_LIMIT = 56 * 1024 * 1024

IN_TM = 1024
ML_TL = 1024
ML_C = 256
SB_TQ = 256
SB_TK = 256
S5_TL = 1024
FFN_TM = 512
FFN_FC = 256


def _const_spec(shape):
    nd = len(shape)
    return pl.BlockSpec(shape, lambda *_: (0,) * nd, pipeline_mode=pl.Buffered(1))


def _dot(a, b):
    return jnp.dot(a, b, preferred_element_type=F32)


def _dot_nt(a, b):
    return lax.dot_general(a, b, (((1,), (1,)), ((), ())), preferred_element_type=F32)


def _dot_tn(a, b):
    return lax.dot_general(a, b, (((0,), (0,)), ((), ())), preferred_element_type=F32)


def _split(x):
    hi = x.astype(BF16)
    lo = (x - hi.astype(F32)).astype(BF16)
    return hi, lo


def _log_sigmoid(x):
    return jnp.minimum(x, 0.0) - jnp.log(1.0 + jnp.exp(-jnp.abs(x)))


def _sigmoid(x):
    return 1.0 / (1.0 + jnp.exp(-x))


def _in_proj_kernel(x_ref, g_ref, w_ref, seg_ref, qg_ref, kg_ref,
                    qk_ref, v_ref, o_ref, sq_ref, sk_ref, sv_ref, u_ref,
                    gr1_ref, gr2_ref, gc1_ref, gc2_ref):
    x = x_ref[0]
    ms = jnp.mean(x * x, axis=-1, keepdims=True)
    h = (x * lax.rsqrt(ms + EPS) * g_ref[...]).astype(BF16)

    def proj(a, b):
        return _dot(h, w_ref[:, a:b])

    def qk_norm(z, gain):
        ss = _dot((z * z).astype(BF16), seg_ref[...])
        return (z * lax.rsqrt(ss * (1.0 / SB_DH) + EPS) * gain).astype(BF16)

    qk_ref[0] = proj(C_QK, C_V)
    zv = proj(C_V, C_O)
    v_ref[0] = zv.astype(BF16)
    o_ref[0] = proj(C_O, C_SQ).astype(BF16)
    sq_ref[0] = qk_norm(proj(C_SQ, C_SK), qg_ref[...])
    sk_ref[0] = qk_norm(proj(C_SK, C_SV), kg_ref[...])
    sv_ref[0] = proj(C_SV, C_U).astype(BF16)
    zu = proj(C_U, C_END)
    nseg = zu.shape[0] // 8
    for r in range(8):
        for hf in range(S5_W // 128):
            u_ref[0, hf, pl.ds(r, nseg, stride=8), :] = zu[r * nseg:(r + 1) * nseg, hf * 128:(hf + 1) * 128]
    gates = slice(GATE_LANE, GATE_LANE + 8)
    z1 = zv[:, :ML_DHP]
    z2 = zv[:, ML_DHP:2 * ML_DHP]
    gc1_ref[0] = z1[:, gates]
    gc2_ref[0] = z2[:, gates]
    gr1_ref[0] = z1.T[gates, :]
    gr2_ref[0] = z2.T[gates, :]


def _in_proj(x, g, w, seg, qg, kg):
    B, L, _ = x.shape
    tm = min(IN_TM, L)
    grid = (B, L // tm)
    tok = lambda w_, dt: jax.ShapeDtypeStruct((B, L, w_), dt)
    tspec = lambda w_: pl.BlockSpec((1, tm, w_), lambda b, i: (b, i, 0))
    rspec = pl.BlockSpec((1, 8, tm), lambda b, i: (b, 0, i))
    return pl.pallas_call(
        _in_proj_kernel,
        grid=grid,
        in_specs=[tspec(D_MODEL), _const_spec((1, D_MODEL)), _const_spec((D_MODEL, C_END)),
                  _const_spec((SB_W, SB_W)), _const_spec((1, SB_W)), _const_spec((1, SB_W))],
        out_specs=[tspec(2 * ML_WP), tspec(ML_WP), tspec(ML_WP), tspec(SB_W), tspec(SB_W), tspec(SB_W),
                   pl.BlockSpec((1, S5_W // 128, tm, 128), lambda b, i: (b, 0, i, 0)),
                   rspec, rspec, tspec(8), tspec(8)],
        out_shape=[tok(2 * ML_WP, F32), tok(ML_WP, BF16), tok(ML_WP, BF16), tok(SB_W, BF16),
                   tok(SB_W, BF16), tok(SB_W, BF16), jax.ShapeDtypeStruct((B, S5_W // 128, L, 128), F32),
                   jax.ShapeDtypeStruct((B, 8, L), F32), jax.ShapeDtypeStruct((B, 8, L), F32),
                   tok(8, F32), tok(8, F32)],
        compiler_params=pltpu.CompilerParams(
            dimension_semantics=("parallel", "parallel"), vmem_limit_bytes=VMEM_LIMIT),
        name="in_proj",
    )(x, g, w, seg, qg, kg)


def _mlstm_kernel(qk_ref, v_ref, o_ref, gr1_ref, gr2_ref, gc1_ref, gc2_ref,
                  cw_ref, cb_ref, gbr1_ref, gbr2_ref, gbc1_ref, gbc2_ref, og_ref,
                  out_ref, ext_s, st_s, m_s, *, tl):
    l = pl.program_id(1)

    @pl.when(l == 0)
    def _():
        ext_s[0:8, :] = jnp.zeros((8, 2 * ML_WP), F32)
        st_s[...] = jnp.zeros_like(st_s)
        m_s[...] = jnp.zeros_like(m_s)

    pre = qk_ref[0]
    ext_s[8:8 + tl, :] = pre
    conv = (cw_ref[3:4, :] * pre + cw_ref[2:3, :] * ext_s[7:7 + tl, :]
            + cw_ref[1:2, :] * ext_s[6:6 + tl, :] + cw_ref[0:1, :] * ext_s[5:5 + tl, :] + cb_ref[...])
    ext_s[0:8, :] = pre[tl - 8:tl, :]
    qk = conv * _sigmoid(conv)
    q_all = qk[:, :ML_WP].astype(BF16)
    k_all = qk[:, ML_WP:] * (ML_DH ** -0.5)

    gr1 = gr1_ref[0] + gbr1_ref[...]
    gr2 = gr2_ref[0] + gbr2_ref[...]
    gc1 = gc1_ref[0] + gbc1_ref[...]
    gc2 = gc2_ref[0] + gbc2_ref[...]

    ri = lax.broadcasted_iota(jnp.int32, (ML_C, ML_C), 0)
    ci = lax.broadcasted_iota(jnp.int32, (ML_C, ML_C), 1)
    visible = ri <= ci
    tri_low = jnp.where(ci <= ri, 1.0, 0.0).astype(BF16)
    tri_up = jnp.where(visible, 1.0, 0.0).astype(BF16)
    lane = lax.broadcasted_iota(jnp.int32, (ML_C, ML_DHP), 1)

    for c in range(tl // ML_C):
        rows = slice(c * ML_C, (c + 1) * ML_C)
        lf_hi, lf_lo = _split(_log_sigmoid(gc2[rows, :]))
        r_col = gc1[rows, :] - (_dot(tri_low, lf_hi) + _dot(tri_low, lf_lo))
        ir = gr1[:, rows]
        lr_hi, lr_lo = _split(_log_sigmoid(gr2[:, rows]))
        b_row = _dot(lr_hi, tri_up) + _dot(lr_lo, tri_up)
        btot = b_row[:, ML_C - 1:ML_C]
        mp = m_s[...]
        wlog = btot - b_row + ir
        m_loc = jnp.max(wlog, axis=-1, keepdims=True)
        wexp = jnp.exp(wlog - m_loc)
        m_new = jnp.maximum(btot + mp, m_loc)
        a_sc = jnp.exp(btot + mp - m_new)
        c_sc = jnp.exp(m_loc - m_new)
        inter_log = b_row + mp

        for hd in range(ML_HEADS):
            cols = slice(hd * ML_DHP, (hd + 1) * ML_DHP)
            head = slice(hd, hd + 1)
            qh = q_all[rows, cols]
            kh = k_all[rows, cols].astype(BF16)
            dlog = jnp.where(visible, r_col[:, head] + b_row[head, :], NEG)
            m_t = jnp.maximum(inter_log[head, :], jnp.max(dlog, axis=0, keepdims=True))
            p_t = (_dot_nt(kh, qh) * jnp.exp(dlog - m_t)).astype(BF16)
            v_t = jnp.where(lane == ML_DH, 1.0, v_ref[0, rows, cols].astype(F32)).T
            st = st_s[hd]
            isc = jnp.exp(inter_log[head, :] - m_t)
            acc = _dot(v_t.astype(BF16), p_t) + isc * _dot_nt(st.astype(BF16), qh)
            den = acc[ML_DH:ML_DH + 1, :]
            hv = acc[:ML_DH, :] * (1.0 / jnp.maximum(jnp.abs(den), jnp.exp(-m_t)))
            ss = jnp.sum(hv * hv, axis=0, keepdims=True) * (1.0 / ML_DH)
            hn_t = jnp.concatenate([hv * lax.rsqrt(ss + EPS), jnp.zeros((ML_DHP - ML_DH, ML_C), F32)], axis=0)
            hn = hn_t.T * og_ref[:, cols] * _sigmoid(o_ref[0, rows, cols].astype(F32))
            out_ref[0, rows, cols] = hn.astype(BF16)
            vw = (v_t * wexp[head, :]).astype(BF16)
            st_s[hd] = a_sc[head, :] * st + c_sc[head, :] * _dot(vw, kh)
        m_s[...] = m_new


def _mlstm(qk, v, o, gr1, gr2, gc1, gc2, cw, cb, gbr1, gbr2, gbc1, gbc2, og):
    B, L, _ = qk.shape
    tl = min(ML_TL, L)
    grid = (B, L // tl)
    tspec = lambda w_: pl.BlockSpec((1, tl, w_), lambda b, i: (b, i, 0))
    rspec = pl.BlockSpec((1, 8, tl), lambda b, i: (b, 0, i))
    return pl.pallas_call(
        functools.partial(_mlstm_kernel, tl=tl),
        grid=grid,
        in_specs=[tspec(2 * ML_WP), tspec(ML_WP), tspec(ML_WP), rspec, rspec, tspec(8), tspec(8),
                  _const_spec((ML_CONV, 2 * ML_WP)), _const_spec((1, 2 * ML_WP)),
                  _const_spec((8, 1)), _const_spec((8, 1)), _const_spec((1, 8)), _const_spec((1, 8)),
                  _const_spec((1, ML_WP))],
        out_specs=tspec(ML_WP),
        out_shape=jax.ShapeDtypeStruct((B, L, ML_WP), BF16),
        scratch_shapes=[pltpu.VMEM((tl + 8, 2 * ML_WP), F32),
                        pltpu.VMEM((ML_HEADS, ML_DHP, ML_DHP), F32),
                        pltpu.VMEM((8, 1), F32)],
        compiler_params=pltpu.CompilerParams(
            dimension_semantics=("parallel", "arbitrary"), vmem_limit_bytes=VMEM_LIMIT),
        name="mlstm",
    )(qk, v, o, gr1, gr2, gc1, gc2, cw, cb, gbr1, gbr2, gbc1, gbc2, og)


def _sb_kernel(stop_ref, q_ref, k_ref, v_ref, og_ref, out_ref, acc_s, r_s):
    i = pl.program_id(1)
    npair = SB_HEADS // 2
    lane = lax.broadcasted_iota(jnp.int32, (SB_TQ, 2 * SB_DH), 1)
    first = lane < SB_DH
    q_heads = []
    for p in range(npair):
        q = q_ref[0, :, p * 128:(p + 1) * 128]
        zero = jnp.zeros_like(q)
        q_heads.append(jnp.where(first, q, zero))
        q_heads.append(jnp.where(first, zero, q))
    ri = lax.broadcasted_iota(jnp.int32, (SB_TQ, SB_TK), 0)
    ci = lax.broadcasted_iota(jnp.int32, (SB_TQ, SB_TK), 1)
    rj = lax.broadcasted_iota(jnp.int32, (SB_TK, SB_TK), 0)
    cj = lax.broadcasted_iota(jnp.int32, (SB_TK, SB_TK), 1)
    suffix = jnp.where(rj >= cj, 1.0, 0.0).astype(BF16)
    sign = jnp.uint32(0x80000000)

    acc_s[...] = jnp.zeros_like(acc_s)
    r_s[...] = jnp.zeros_like(r_s)

    def blocks(tiles):
        units = [(t, hd) for t in range(len(tiles)) for hd in range(SB_HEADS)]
        ks, vs, zs, sbs = {}, {}, {}, {}
        for t, (j, _) in enumerate(tiles):
            start = pl.multiple_of(j * SB_TK, SB_TK)
            for p in range(npair):
                ks[t, p] = k_ref[0, pl.ds(start, SB_TK), p * 128:(p + 1) * 128]
                vs[t, p] = v_ref[0, pl.ds(start, SB_TK), p * 128:(p + 1) * 128]

        def scores(u):
            t, hd = u
            zs[u] = _dot_nt(q_heads[hd], ks[t, hd // 2])

        def suffix_sums(u):
            causal = tiles[u[0]][1]
            z = zs[u]
            neg_abs = lax.bitcast_convert_type(lax.bitcast_convert_type(z, jnp.uint32) | sign, F32)
            sp = jnp.maximum(z, 0.0) + jnp.log2(1.0 + jnp.exp2(neg_abs))
            if causal is not None:
                sp = jnp.where(causal, sp, 0.0)
            sbs[u] = _dot(sp.astype(BF16), suffix)

        def weights_and_values(u):
            t, hd = u
            causal = tiles[t][1]
            r = r_s[hd]
            a = jnp.exp2(zs[u] - (sbs[u] + r))
            if causal is not None:
                a = jnp.where(causal, a, 0.0)
            r_s[hd] = r + sbs[u][:, 0:1]
            acc_s[hd] += _dot(a.astype(BF16), vs[t, hd // 2])

        stages = (scores, suffix_sums, weights_and_values)
        for step in range(len(units) + len(stages) - 1):
            for lag, stage in enumerate(stages):
                if 0 <= step - lag < len(units):
                    stage(units[step - lag])

    nd = SB_TQ // SB_TK
    blocks([(i * nd + d, ci + d * SB_TK < ri) for d in reversed(range(nd))])

    def min_suffix():
        m = r_s[0]
        for hd in range(1, SB_HEADS):
            m = jnp.minimum(m, r_s[hd])
        return jnp.min(m)

    n_left = i * nd

    def cond(carry):
        t, rmin = carry
        return jnp.logical_and(t < n_left, rmin <= stop_ref[0])

    def body(carry):
        t, _ = carry
        blocks([(n_left - 1 - t, None)])
        return t + 1, min_suffix()

    lax.while_loop(cond, body, (jnp.int32(0), min_suffix()))

    for p in range(npair):
        o = jnp.where(first, acc_s[2 * p], acc_s[2 * p + 1])
        o2 = o * o
        ss_a = jnp.sum(jnp.where(first, o2, 0.0), axis=-1, keepdims=True)
        ss_b = jnp.sum(jnp.where(first, 0.0, o2), axis=-1, keepdims=True)
        ss = jnp.where(first, ss_a, ss_b) * (1.0 / SB_DH)
        out_ref[0, :, p * 128:(p + 1) * 128] = (
            o * lax.rsqrt(ss + EPS) * og_ref[:, p * 128:(p + 1) * 128]).astype(BF16)


def _sb_attn(stop, q, k, v, og):
    B, L, _ = q.shape
    assert L % SB_TQ == 0 and SB_TQ % SB_TK == 0
    grid = (B, L // SB_TQ)
    return pl.pallas_call(
        _sb_kernel,
        grid=grid,
        in_specs=[pl.BlockSpec(memory_space=pltpu.SMEM),
                  pl.BlockSpec((1, SB_TQ, SB_W), lambda b, i: (b, i, 0)),
                  pl.BlockSpec((1, L, SB_W), lambda b, i: (b, 0, 0)),
                  pl.BlockSpec((1, L, SB_W), lambda b, i: (b, 0, 0)),
                  _const_spec((1, SB_W))],
        out_specs=pl.BlockSpec((1, SB_TQ, SB_W), lambda b, i: (b, i, 0)),
        out_shape=jax.ShapeDtypeStruct((B, L, SB_W), BF16),
        scratch_shapes=[pltpu.VMEM((SB_HEADS, SB_TQ, 2 * SB_DH), F32),
                        pltpu.VMEM((SB_HEADS, SB_TQ, 1), F32)],
        compiler_params=pltpu.CompilerParams(
            dimension_semantics=("parallel", "parallel"), vmem_limit_bytes=VMEM_LIMIT),
        name="sb_attn",
    )(stop, q, k, v, og)


def _gelu_tanh(x):
    return 0.5 * x * (1.0 + jnp.tanh(math.sqrt(2.0 / math.pi) * (x + 0.044715 * (x * x * x))))


def _cmul(ar, ai, br, bi):
    return ar * br - ai * bi, ar * bi + ai * br


def _s5_kernel(u_ref, bm_ref, cm_ref, pw_ref, d_ref, gw_ref, gb_ref, og_ref, out_ref, xs_s, y_s, carry_s, *, tl):
    l = pl.program_id(1)
    nseg = tl // 8
    nlt = S5_LANES // 128
    grp = 8

    @pl.when(l == 0)
    def _():
        carry_s[...] = jnp.zeros_like(carry_s)

    u = jnp.concatenate([u_ref[0, hf] for hf in range(S5_W // 128)], axis=-1)
    xs_s[...] = _dot(u.astype(BF16), bm_ref[...])

    def lanes(lt):
        return slice(lt * 128, (lt + 1) * 128), slice(S5_LANES + lt * 128, S5_LANES + (lt + 1) * 128)

    for g0 in range(0, nlt, grp):
        tiles = range(g0, g0 + grp)
        lam = [(pw_ref[0, :, lanes(lt)[0]], pw_ref[1, :, lanes(lt)[0]]) for lt in tiles]

        def local_step(j, st):
            r0 = pl.multiple_of(j * 8, 8)
            new = []
            for n, lt in enumerate(tiles):
                re_l, im_l = lanes(lt)
                pr, pi_ = _cmul(lam[n][0], lam[n][1], st[2 * n], st[2 * n + 1])
                xr = xs_s[pl.ds(r0, 8), re_l] + pr
                xi = xs_s[pl.ds(r0, 8), im_l] + pi_
                xs_s[pl.ds(r0, 8), re_l] = xr
                xs_s[pl.ds(r0, 8), im_l] = xi
                new += [xr, xi]
            return tuple(new)

        fin = lax.fori_loop(0, nseg, local_step, tuple(jnp.zeros((8, 128), F32) for _ in range(2 * grp)),
                            unroll=4)

        start = []
        for n, lt in enumerate(tiles):
            re_l, im_l = lanes(lt)
            er, ei = fin[2 * n], fin[2 * n + 1]
            for m, k in enumerate((1, 2, 4)):
                pr, pi_ = _cmul(pw_ref[2 + 2 * m, :, re_l], pw_ref[3 + 2 * m, :, re_l],
                                pltpu.roll(er, k, 0), pltpu.roll(ei, k, 0))
                er, ei = er + pr, ei + pi_
            s0r, s0i = carry_s[:, re_l], carry_s[:, im_l]
            pr, pi_ = _cmul(pw_ref[8, :, re_l], pw_ref[9, :, re_l], s0r, s0i)
            er, ei = er + pr, ei + pi_
            carry_s[:, re_l] = jnp.broadcast_to(er[7:8, :], (8, 128))
            carry_s[:, im_l] = jnp.broadcast_to(ei[7:8, :], (8, 128))
            row = lax.broadcasted_iota(jnp.int32, (8, 128), 0)
            start += [jnp.where(row == 0, s0r, pltpu.roll(er, 1, 0)),
                      jnp.where(row == 0, s0i, pltpu.roll(ei, 1, 0))]

        def fix_step(j, tr):
            r0 = pl.multiple_of(j * 8, 8)
            new = []
            for n, lt in enumerate(tiles):
                re_l, im_l = lanes(lt)
                pr, pi_ = _cmul(lam[n][0], lam[n][1], tr[2 * n], tr[2 * n + 1])
                xs_s[pl.ds(r0, 8), re_l] += pr
                xs_s[pl.ds(r0, 8), im_l] += pi_
                new += [pr, pi_]
            return tuple(new)

        lax.fori_loop(0, nseg, fix_step, tuple(start), unroll=4)

    rb = min(256, tl)
    for r0 in range(0, tl, rb):
        rows = slice(r0, r0 + rb)
        y = _dot(xs_s[rows, :].astype(BF16), cm_ref[...]) + d_ref[...] * u[rows, :]
        y = _gelu_tanh(y)
        y = y * _sigmoid(_dot(y.astype(BF16), gw_ref[...]) + gb_ref[...])
        ms = jnp.mean(y * y, axis=-1, keepdims=True)
        y = y * lax.rsqrt(ms + EPS) * og_ref[...]
        for hf in range(S5_W // 128):
            y_s[hf, rows, :] = y[:, hf * 128:(hf + 1) * 128]
    for r in range(8):
        out_ref[0, r * nseg:(r + 1) * nseg, :] = jnp.concatenate(
            [y_s[hf, pl.ds(r, nseg, stride=8), :] for hf in range(S5_W // 128)], axis=-1).astype(BF16)


def _s5(u, bm, cm, pw, d, gw, gb, og):
    B, _, L, _ = u.shape
    tl = min(S5_TL, L)
    grid = (B, L // tl)
    tspec = pl.BlockSpec((1, tl, S5_W), lambda b, i: (b, i, 0))
    uspec = pl.BlockSpec((1, S5_W // 128, tl, 128), lambda b, i: (b, 0, i, 0))
    return pl.pallas_call(
        functools.partial(_s5_kernel, tl=tl),
        grid=grid,
        in_specs=[uspec, _const_spec((S5_W, 2 * S5_LANES)), _const_spec((2 * S5_LANES, S5_W)),
                  _const_spec((10, 8, S5_LANES)), _const_spec((1, S5_W)), _const_spec((S5_W, S5_W)),
                  _const_spec((1, S5_W)), _const_spec((1, S5_W))],
        out_specs=tspec,
        out_shape=jax.ShapeDtypeStruct((B, L, S5_W), BF16),
        scratch_shapes=[pltpu.VMEM((tl, 2 * S5_LANES), F32), pltpu.VMEM((S5_W // 128, tl, 128), F32),
                        pltpu.VMEM((8, 2 * S5_LANES), F32)],
        compiler_params=pltpu.CompilerParams(
            dimension_semantics=("parallel", "arbitrary"), vmem_limit_bytes=VMEM_LIMIT),
        name="s5",
    )(u, bm, cm, pw, d, gw, gb, og)


def _ffn_kernel(x_ref, hm_ref, hs_ref, h5_ref, wom_ref, wos_ref, wo5_ref, g_ref,
                wu_ref, cw_ref, wd_ref, out_ref, h_s, gext_s, act_s, carry_s, *, tm):
    l = pl.program_id(1)

    @pl.when(l == 0)
    def _():
        carry_s[...] = jnp.zeros_like(carry_s)

    x1 = (x_ref[0] + _dot(hm_ref[0], wom_ref[...]) + _dot(hs_ref[0], wos_ref[...])
          + _dot(h5_ref[0], wo5_ref[...]))
    out_ref[0] = x1
    ms = jnp.mean(x1 * x1, axis=-1, keepdims=True)
    h_s[...] = (x1 * lax.rsqrt(ms + EPS) * g_ref[...]).astype(BF16)

    for c in range(D_FF // FFN_FC):
        cols = slice(c * FFN_FC, (c + 1) * FFN_FC)
        g = _dot(h_s[...], wu_ref[:, cols])
        v = _dot(h_s[...], wu_ref[:, D_FF + c * FFN_FC:D_FF + (c + 1) * FFN_FC])
        gext_s[0:8, :] = carry_s[c]
        gext_s[8:8 + tm, :] = g
        carry_s[c] = g[tm - 8:tm, :]
        conv = (cw_ref[2:3, cols] * g + cw_ref[1:2, cols] * gext_s[7:7 + tm, :]
                + cw_ref[0:1, cols] * gext_s[6:6 + tm, :])
        act_s[:, cols] = (conv * _sigmoid(conv) * v).astype(BF16)

    out_ref[0] += _dot(act_s[...], wd_ref[...])


def _ffn(x, hm, hs, h5, wom, wos, wo5, g, wu, cw, wd):
    B, L, _ = x.shape
    tm = min(FFN_TM, L)
    grid = (B, L // tm)
    tspec = lambda w_: pl.BlockSpec((1, tm, w_), lambda b, i: (b, i, 0))
    return pl.pallas_call(
        functools.partial(_ffn_kernel, tm=tm),
        grid=grid,
        in_specs=[tspec(D_MODEL), tspec(ML_WP), tspec(SB_W), tspec(S5_W),
                  _const_spec((ML_WP, D_MODEL)), _const_spec((SB_W, D_MODEL)), _const_spec((S5_W, D_MODEL)),
                  _const_spec((1, D_MODEL)), _const_spec((D_MODEL, 2 * D_FF)),
                  _const_spec((3, D_FF)), _const_spec((D_FF, D_MODEL))],
        out_specs=tspec(D_MODEL),
        out_shape=jax.ShapeDtypeStruct((B, L, D_MODEL), F32),
        scratch_shapes=[pltpu.VMEM((tm, D_MODEL), BF16), pltpu.VMEM((tm + 8, FFN_FC), F32),
                        pltpu.VMEM((tm, D_FF), BF16), pltpu.VMEM((D_FF // FFN_FC, 8, FFN_FC), F32)],
        compiler_params=pltpu.CompilerParams(
            dimension_semantics=("parallel", "arbitrary"), vmem_limit_bytes=VMEM_LIMIT),
        name="ffn",
    )(x, hm, hs, h5, wom, wos, wo5, g, wu, cw, wd)


def _pad_heads(w):
    lead = w.shape[:-1]
    w = w.reshape(lead + (ML_HEADS, ML_DH))
    w = jnp.pad(w, [(0, 0)] * len(lead) + [(0, 0), (0, ML_DHP - ML_DH)])
    return w.reshape(lead + (ML_WP,))


def _s5_params(a_re, a_im, log_dt, b_re, b_im, c_re, c_im, nseg):
    lr = jnp.minimum(a_re, -1e-4)
    li = a_im
    dt = jnp.exp(log_dt)[:, None]
    mag = jnp.exp(lr * dt)
    br = mag * jnp.cos(li * dt)
    bi = mag * jnp.sin(li * dt)
    nr, ni = br - 1.0, bi
    den = lr * lr + li * li
    fr = (nr * lr + ni * li) / den
    fi = (ni * lr - nr * li) / den
    bbr = fr[..., None] * b_re - fi[..., None] * b_im
    bbi = fr[..., None] * b_im + fi[..., None] * b_re
    eye = jnp.eye(S5_GROUPS, dtype=F32)
    bm_re = jnp.einsum("gk,gph->ghkp", eye, bbr).reshape(S5_W, S5_LANES)
    bm_im = jnp.einsum("gk,gph->ghkp", eye, bbi).reshape(S5_W, S5_LANES)
    bm = jnp.concatenate([bm_re, bm_im], axis=1).astype(BF16)
    cm_re = jnp.einsum("gk,ghp->gpkh", eye, c_re).reshape(S5_LANES, S5_W)
    cm_im = jnp.einsum("gk,ghp->gpkh", eye, c_im).reshape(S5_LANES, S5_W)
    cm = jnp.concatenate([cm_re, -cm_im], axis=0).astype(BF16)
    def cpow2(r, i, n):
        for _ in range(n):
            r, i = r * r - i * i, 2.0 * r * i
        return r, i

    lr_, li_ = br.reshape(-1), bi.reshape(-1)
    assert nseg & (nseg - 1) == 0
    ar, ai = cpow2(lr_, li_, nseg.bit_length() - 1)
    pr, pi_ = [ar], [ai]
    for _ in range(7):
        r, i = pr[-1], pi_[-1]
        pr.append(r * ar - i * ai)
        pi_.append(r * ai + i * ar)
    row = jnp.arange(8)[:, None]
    planes = [jnp.broadcast_to(lr_[None, :], (8, S5_LANES)), jnp.broadcast_to(li_[None, :], (8, S5_LANES))]
    for k in (1, 2, 4):
        keep = row >= k
        planes.append(jnp.where(keep, pr[k - 1][None, :], 0.0))
        planes.append(jnp.where(keep, pi_[k - 1][None, :], 0.0))
    planes.append(jnp.stack(pr, axis=0))
    planes.append(jnp.stack(pi_, axis=0))
    pw = jnp.stack(planes, axis=0).astype(F32)
    return bm, cm, pw


def kernel(x, norm_mix_g, w_in, ml_conv_w, ml_conv_b, ml_gate_b, ml_out_g, sb_q_g, sb_k_g, sb_out_g, s5_a_re, s5_a_im, s5_log_dt, s5_b_re, s5_b_im, s5_c_re, s5_c_im, s5_d, s5_glu_w, s5_glu_b, s5_out_g, w_out, norm_ffn_g, ffn_w_up, ffn_conv_w, ffn_w_down):
    depth = w_in.shape[0]
    w_up_b = ffn_w_up.astype(BF16)
    w_down_b = ffn_w_down.astype(BF16)
    w_out_b = w_out.astype(BF16)
    bm_all, cm_all, pw_all = jax.vmap(functools.partial(_s5_params, nseg=min(S5_TL, x.shape[1]) // 8))(
        s5_a_re, s5_a_im, s5_log_dt, s5_b_re, s5_b_im, s5_c_re, s5_c_im)
    seg = jnp.kron(jnp.eye(SB_HEADS, dtype=F32), jnp.ones((SB_DH, SB_DH), F32)).astype(BF16)
    for l in range(depth):
        w = w_in[l]
        o0 = 0
        parts = []
        for wd_ in (ML_W, ML_W, ML_W, ML_W, 2 * ML_HEADS, SB_W, SB_W, SB_W, S5_W):
            parts.append(w[:, o0:o0 + wd_])
            o0 += wd_
        wq, wk, wv_, wo_, wif, wsq, wsk, wsv, wu = parts
        wfi = jnp.concatenate([wif[:, ML_HEADS:], wif[:, :ML_HEADS]], axis=1)
        wv_pad = _pad_heads(wv_)
        wv_pad = wv_pad.at[:, GATE_LANE:GATE_LANE + 8].set(wif)
        wv_pad = wv_pad.at[:, ML_DHP + GATE_LANE:ML_DHP + GATE_LANE + 8].set(wfi)
        w_all = jnp.concatenate(
            [_pad_heads(wq), _pad_heads(wk), wv_pad, _pad_heads(wo_), wsq, wsk, wsv, wu], axis=1).astype(BF16)
        qg = jnp.tile(sb_q_g[l], SB_HEADS)[None, :] * (SB_DH ** -0.5 * math.log2(math.e))
        kg = jnp.tile(sb_k_g[l], SB_HEADS)[None, :]
        (qk, mv, mo, sq, sk, sv, u, gr1, gr2, gc1, gc2) = _in_proj(
            x, norm_mix_g[l][None, :], w_all, seg, qg, kg)

        cw = jnp.concatenate([_pad_heads(ml_conv_w[l][:, :ML_W]), _pad_heads(ml_conv_w[l][:, ML_W:])], axis=1)
        cb = jnp.concatenate([_pad_heads(ml_conv_b[l][:ML_W]), _pad_heads(ml_conv_b[l][ML_W:])])[None, :]
        gb = ml_gate_b[l]
        gb2 = jnp.concatenate([gb[ML_HEADS:], gb[:ML_HEADS]])
        hm = _mlstm(qk, mv, mo, gr1, gr2, gc1, gc2, cw, cb, gb[:, None], gb2[:, None], gb[None, :],
                    gb2[None, :], _pad_heads(ml_out_g[l].reshape(-1))[None, :])

        stop = (1.05 * SB_DH * jnp.max(jnp.abs(qg)) * jnp.max(jnp.abs(kg)) + 160.0).reshape(1).astype(F32)
        hs = _sb_attn(stop, sq, sk, sv, sb_out_g[l].reshape(1, SB_W))

        h5 = _s5(u, bm_all[l], cm_all[l], pw_all[l], s5_d[l].reshape(1, S5_W), s5_glu_w[l].astype(BF16),
                 s5_glu_b[l][None, :], s5_out_g[l][None, :])

        wo = w_out_b[l]
        wom = jnp.pad(wo[:ML_W].reshape(ML_HEADS, ML_DH, D_MODEL),
                      [(0, 0), (0, ML_DHP - ML_DH), (0, 0)]).reshape(ML_WP, D_MODEL)
        x = _ffn(x, hm, hs, h5, wom, wo[ML_W:ML_W + SB_W], wo[ML_W + SB_W:], norm_ffn_g[l][None, :],
                 w_up_b[l], ffn_conv_w[l], w_down_b[l])
    return x
```

```python
import functools
import math

import jax
import jax.numpy as jnp
from jax import lax
from jax.experimental import pallas as pl
from jax.experimental.pallas import tpu as pltpu

F32 = jnp.float32
BF16 = jnp.bfloat16

D_MODEL = 1024
ML_HEADS = 4
ML_DH = 96
ML_DHP = 128
ML_WP = ML_HEADS * ML_DHP
ML_W = ML_HEADS * ML_DH
ML_CONV = 4
SB_HEADS = 6
SB_DH = 64
SB_W = SB_HEADS * SB_DH
S5_W = 256
S5_GROUPS = 16
S5_CH = 16
S5_STATE = 64
S5_LANES = S5_GROUPS * S5_STATE
D_FF = 2816
EPS = 1e-6
NEG = -1e30

C_QK, C_V, C_O, C_SQ, C_SK, C_SV, C_U, C_END = (0, 1024, 1536, 2048, 2432, 2816, 3200, 3456)
GATE_LANE = 104

VMEM_LIMIT = 56 * 1024 * 1024

IN_TM = 1024
ML_TL = 1024
ML_C = 256
SB_TQ = 256
SB_TK = 256
S5_TL = 1024
FFN_TM = 512
FFN_FC = 256


def _const_spec(shape):
    nd = len(shape)
    return pl.BlockSpec(shape, lambda *_: (0,) * nd, pipeline_mode=pl.Buffered(1))


def _dot(a, b):
    return jnp.dot(a, b, preferred_element_type=F32)


def _dot_nt(a, b):
    return lax.dot_general(a, b, (((1,), (1,)), ((), ())), preferred_element_type=F32)


def _dot_tn(a, b):
    return lax.dot_general(a, b, (((0,), (0,)), ((), ())), preferred_element_type=F32)


def _split(x):
    hi = x.astype(BF16)
    lo = (x - hi.astype(F32)).astype(BF16)
    return hi, lo


def _log_sigmoid(x):
    return jnp.minimum(x, 0.0) - jnp.log(1.0 + jnp.exp(-jnp.abs(x)))


def _sigmoid(x):
    return 1.0 / (1.0 + jnp.exp(-x))


def _in_proj_kernel(x_ref, g_ref, w_ref, seg_ref, qg_ref, kg_ref,
                    qk_ref, v_ref, o_ref, sq_ref, sk_ref, sv_ref, u_ref,
                    gr1_ref, gr2_ref, gc1_ref, gc2_ref):
    x = x_ref[0]
    ms = jnp.mean(x * x, axis=-1, keepdims=True)
    h = (x * lax.rsqrt(ms + EPS) * g_ref[...]).astype(BF16)

    def proj(a, b):
        return _dot(h, w_ref[:, a:b])

    def qk_norm(z, gain):
        ss = _dot((z * z).astype(BF16), seg_ref[...])
        return (z * lax.rsqrt(ss * (1.0 / SB_DH) + EPS) * gain).astype(BF16)

    qk_ref[0] = proj(C_QK, C_V)
    zv = proj(C_V, C_O)
    v_ref[0] = zv.astype(BF16)
    o_ref[0] = proj(C_O, C_SQ).astype(BF16)
    sq_ref[0] = qk_norm(proj(C_SQ, C_SK), qg_ref[...])
    sk_ref[0] = qk_norm(proj(C_SK, C_SV), kg_ref[...])
    sv_ref[0] = proj(C_SV, C_U).astype(BF16)
    zu = proj(C_U, C_END)
    nseg = zu.shape[0] // 8
    for r in range(8):
        for hf in range(S5_W // 128):
            u_ref[0, hf, pl.ds(r, nseg, stride=8), :] = zu[r * nseg:(r + 1) * nseg, hf * 128:(hf + 1) * 128]
    gates = slice(GATE_LANE, GATE_LANE + 8)
    z1 = zv[:, :ML_DHP]
    z2 = zv[:, ML_DHP:2 * ML_DHP]
    gc1_ref[0] = z1[:, gates]
    gc2_ref[0] = z2[:, gates]
    gr1_ref[0] = z1.T[gates, :]
    gr2_ref[0] = z2.T[gates, :]


def _in_proj(x, g, w, seg, qg, kg):
    B, L, _ = x.shape
    tm = min(IN_TM, L)
    grid = (B, L // tm)
    tok = lambda w_, dt: jax.ShapeDtypeStruct((B, L, w_), dt)
    tspec = lambda w_: pl.BlockSpec((1, tm, w_), lambda b, i: (b, i, 0))
    rspec = pl.BlockSpec((1, 8, tm), lambda b, i: (b, 0, i))
    return pl.pallas_call(
        _in_proj_kernel,
        grid=grid,
        in_specs=[tspec(D_MODEL), _const_spec((1, D_MODEL)), _const_spec((D_MODEL, C_END)),
                  _const_spec((SB_W, SB_W)), _const_spec((1, SB_W)), _const_spec((1, SB_W))],
        out_specs=[tspec(2 * ML_WP), tspec(ML_WP), tspec(ML_WP), tspec(SB_W), tspec(SB_W), tspec(SB_W),
                   pl.BlockSpec((1, S5_W // 128, tm, 128), lambda b, i: (b, 0, i, 0)),
                   rspec, rspec, tspec(8), tspec(8)],
        out_shape=[tok(2 * ML_WP, F32), tok(ML_WP, BF16), tok(ML_WP, BF16), tok(SB_W, BF16),
                   tok(SB_W, BF16), tok(SB_W, BF16), jax.ShapeDtypeStruct((B, S5_W // 128, L, 128), F32),
                   jax.ShapeDtypeStruct((B, 8, L), F32), jax.ShapeDtypeStruct((B, 8, L), F32),
                   tok(8, F32), tok(8, F32)],
        compiler_params=pltpu.CompilerParams(
            dimension_semantics=("parallel", "parallel"), vmem_limit_bytes=VMEM_LIMIT),
        name="in_proj",
    )(x, g, w, seg, qg, kg)


def _mlstm_kernel(qk_ref, v_ref, o_ref, gr1_ref, gr2_ref, gc1_ref, gc2_ref,
                  cw_ref, cb_ref, gbr1_ref, gbr2_ref, gbc1_ref, gbc2_ref, og_ref,
                  out_ref, ext_s, st_s, m_s, *, tl):
    l = pl.program_id(1)

    @pl.when(l == 0)
    def _():
        ext_s[0:8, :] = jnp.zeros((8, 2 * ML_WP), F32)
        st_s[...] = jnp.zeros_like(st_s)
        m_s[...] = jnp.zeros_like(m_s)

    pre = qk_ref[0]
    ext_s[8:8 + tl, :] = pre
    conv = (cw_ref[3:4, :] * pre + cw_ref[2:3, :] * ext_s[7:7 + tl, :]
            + cw_ref[1:2, :] * ext_s[6:6 + tl, :] + cw_ref[0:1, :] * ext_s[5:5 + tl, :] + cb_ref[...])
    ext_s[0:8, :] = pre[tl - 8:tl, :]
    qk = conv * _sigmoid(conv)
    q_all = qk[:, :ML_WP].astype(BF16)
    k_all = qk[:, ML_WP:] * (ML_DH ** -0.5)

    gr1 = gr1_ref[0] + gbr1_ref[...]
    gr2 = gr2_ref[0] + gbr2_ref[...]
    gc1 = gc1_ref[0] + gbc1_ref[...]
    gc2 = gc2_ref[0] + gbc2_ref[...]

    ri = lax.broadcasted_iota(jnp.int32, (ML_C, ML_C), 0)
    ci = lax.broadcasted_iota(jnp.int32, (ML_C, ML_C), 1)
    visible = ri <= ci
    tri_low = jnp.where(ci <= ri, 1.0, 0.0).astype(BF16)
    tri_up = jnp.where(visible, 1.0, 0.0).astype(BF16)
    lane = lax.broadcasted_iota(jnp.int32, (ML_C, ML_DHP), 1)

    for c in range(tl // ML_C):
        rows = slice(c * ML_C, (c + 1) * ML_C)
        lf_hi, lf_lo = _split(_log_sigmoid(gc2[rows, :]))
        r_col = gc1[rows, :] - (_dot(tri_low, lf_hi) + _dot(tri_low, lf_lo))
        ir = gr1[:, rows]
        lr_hi, lr_lo = _split(_log_sigmoid(gr2[:, rows]))
        b_row = _dot(lr_hi, tri_up) + _dot(lr_lo, tri_up)
        btot = b_row[:, ML_C - 1:ML_C]
        mp = m_s[...]
        wlog = btot - b_row + ir
        m_loc = jnp.max(wlog, axis=-1, keepdims=True)
        wexp = jnp.exp(wlog - m_loc)
        m_new = jnp.maximum(btot + mp, m_loc)
        a_sc = jnp.exp(btot + mp - m_new)
        c_sc = jnp.exp(m_loc - m_new)
        inter_log = b_row + mp

        for hd in range(ML_HEADS):
            cols = slice(hd * ML_DHP, (hd + 1) * ML_DHP)
            head = slice(hd, hd + 1)
            qh = q_all[rows, cols]
            kh = k_all[rows, cols].astype(BF16)
            dlog = jnp.where(visible, r_col[:, head] + b_row[head, :], NEG)
            m_t = jnp.maximum(inter_log[head, :], jnp.max(dlog, axis=0, keepdims=True))
            p_t = (_dot_nt(kh, qh) * jnp.exp(dlog - m_t)).astype(BF16)
            v_t = jnp.where(lane == ML_DH, 1.0, v_ref[0, rows, cols].astype(F32)).T
            st = st_s[hd]
            isc = jnp.exp(inter_log[head, :] - m_t)
            acc = _dot(v_t.astype(BF16), p_t) + isc * _dot_nt(st.astype(BF16), qh)
            den = acc[ML_DH:ML_DH + 1, :]
            hv = acc[:ML_DH, :] * (1.0 / jnp.maximum(jnp.abs(den), jnp.exp(-m_t)))
            ss = jnp.sum(hv * hv, axis=0, keepdims=True) * (1.0 / ML_DH)
            hn_t = jnp.concatenate([hv * lax.rsqrt(ss + EPS), jnp.zeros((ML_DHP - ML_DH, ML_C), F32)], axis=0)
            hn = hn_t.T * og_ref[:, cols] * _sigmoid(o_ref[0, rows, cols].astype(F32))
            out_ref[0, rows, cols] = hn.astype(BF16)
            vw = (v_t * wexp[head, :]).astype(BF16)
            st_s[hd] = a_sc[head, :] * st + c_sc[head, :] * _dot(vw, kh)
        m_s[...] = m_new


def _mlstm(qk, v, o, gr1, gr2, gc1, gc2, cw, cb, gbr1, gbr2, gbc1, gbc2, og):
    B, L, _ = qk.shape
    tl = min(ML_TL, L)
    grid = (B, L // tl)
    tspec = lambda w_: pl.BlockSpec((1, tl, w_), lambda b, i: (b, i, 0))
    rspec = pl.BlockSpec((1, 8, tl), lambda b, i: (b, 0, i))
    return pl.pallas_call(
        functools.partial(_mlstm_kernel, tl=tl),
        grid=grid,
        in_specs=[tspec(2 * ML_WP), tspec(ML_WP), tspec(ML_WP), rspec, rspec, tspec(8), tspec(8),
                  _const_spec((ML_CONV, 2 * ML_WP)), _const_spec((1, 2 * ML_WP)),
                  _const_spec((8, 1)), _const_spec((8, 1)), _const_spec((1, 8)), _const_spec((1, 8)),
                  _const_spec((1, ML_WP))],
        out_specs=tspec(ML_WP),
        out_shape=jax.ShapeDtypeStruct((B, L, ML_WP), BF16),
        scratch_shapes=[pltpu.VMEM((tl + 8, 2 * ML_WP), F32),
                        pltpu.VMEM((ML_HEADS, ML_DHP, ML_DHP), F32),
                        pltpu.VMEM((8, 1), F32)],
        compiler_params=pltpu.CompilerParams(
            dimension_semantics=("parallel", "arbitrary"), vmem_limit_bytes=VMEM_LIMIT),
        name="mlstm",
    )(qk, v, o, gr1, gr2, gc1, gc2, cw, cb, gbr1, gbr2, gbc1, gbc2, og)


def _sb_kernel(stop_ref, q_ref, k_ref, v_ref, og_ref, out_ref, acc_s, r_s):
    i = pl.program_id(1)
    npair = SB_HEADS // 2
    lane = lax.broadcasted_iota(jnp.int32, (SB_TQ, 2 * SB_DH), 1)
    first = lane < SB_DH
    q_heads = []
    for p in range(npair):
        q = q_ref[0, :, p * 128:(p + 1) * 128]
        zero = jnp.zeros_like(q)
        q_heads.append(jnp.where(first, q, zero))
        q_heads.append(jnp.where(first, zero, q))
    ri = lax.broadcasted_iota(jnp.int32, (SB_TQ, SB_TK), 0)
    ci = lax.broadcasted_iota(jnp.int32, (SB_TQ, SB_TK), 1)
    rj = lax.broadcasted_iota(jnp.int32, (SB_TK, SB_TK), 0)
    cj = lax.broadcasted_iota(jnp.int32, (SB_TK, SB_TK), 1)
    suffix = jnp.where(rj >= cj, 1.0, 0.0).astype(BF16)
    sign = jnp.uint32(0x80000000)

    acc_s[...] = jnp.zeros_like(acc_s)
    r_s[...] = jnp.zeros_like(r_s)

    def blocks(tiles):
        units = [(t, hd) for t in range(len(tiles)) for hd in range(SB_HEADS)]
        ks, vs, zs, sbs = {}, {}, {}, {}
        for t, (j, _) in enumerate(tiles):
            start = pl.multiple_of(j * SB_TK, SB_TK)
            for p in range(npair):
                ks[t, p] = k_ref[0, pl.ds(start, SB_TK), p * 128:(p + 1) * 128]
                vs[t, p] = v_ref[0, pl.ds(start, SB_TK), p * 128:(p + 1) * 128]

        def scores(u):
            t, hd = u
            zs[u] = _dot_nt(q_heads[hd], ks[t, hd // 2])

        def suffix_sums(u):
            causal = tiles[u[0]][1]
            z = zs[u]
            neg_abs = lax.bitcast_convert_type(lax.bitcast_convert_type(z, jnp.uint32) | sign, F32)
            sp = jnp.maximum(z, 0.0) + jnp.log2(1.0 + jnp.exp2(neg_abs))
            if causal is not None:
                sp = jnp.where(causal, sp, 0.0)
            sbs[u] = _dot(sp.astype(BF16), suffix)

        def weights_and_values(u):
            t, hd = u
            causal = tiles[t][1]
            r = r_s[hd]
            a = jnp.exp2(zs[u] - (sbs[u] + r))
            if causal is not None:
                a = jnp.where(causal, a, 0.0)
            r_s[hd] = r + sbs[u][:, 0:1]
            acc_s[hd] += _dot(a.astype(BF16), vs[t, hd // 2])

        stages = (scores, suffix_sums, weights_and_values)
        for step in range(len(units) + len(stages) - 1):
            for lag, stage in enumerate(stages):
                if 0 <= step - lag < len(units):
                    stage(units[step - lag])

    nd = SB_TQ // SB_TK
    diag = [(i * nd + d, ci + d * SB_TK < ri) for d in reversed(range(nd))]

    @pl.when(i == 0)
    def _():
        blocks(diag)

    @pl.when(i > 0)
    def _():
        blocks(diag + [(i * nd - 1, None)])

    def min_suffix():
        m = r_s[0]
        for hd in range(1, SB_HEADS):
            m = jnp.minimum(m, r_s[hd])
        return jnp.min(m)

    n_left = jnp.maximum(i * nd - 1, 0)

    def cond(carry):
        t, rmin = carry
        return jnp.logical_and(t < n_left, rmin <= stop_ref[0])

    def body(carry):
        t, _ = carry
        blocks([(n_left - 1 - t, None)])
        return t + 1, min_suffix()

    lax.while_loop(cond, body, (jnp.int32(0), min_suffix()))

    for p in range(npair):
        o = jnp.where(first, acc_s[2 * p], acc_s[2 * p + 1])
        o2 = o * o
        ss_a = jnp.sum(jnp.where(first, o2, 0.0), axis=-1, keepdims=True)
        ss_b = jnp.sum(jnp.where(first, 0.0, o2), axis=-1, keepdims=True)
        ss = jnp.where(first, ss_a, ss_b) * (1.0 / SB_DH)
        out_ref[0, :, p * 128:(p + 1) * 128] = (
            o * lax.rsqrt(ss + EPS) * og_ref[:, p * 128:(p + 1) * 128]).astype(BF16)


def _sb_attn(stop, q, k, v, og):
    B, L, _ = q.shape
    assert L % SB_TQ == 0 and SB_TQ % SB_TK == 0
    grid = (B, L // SB_TQ)
    return pl.pallas_call(
        _sb_kernel,
        grid=grid,
        in_specs=[pl.BlockSpec(memory_space=pltpu.SMEM),
                  pl.BlockSpec((1, SB_TQ, SB_W), lambda b, i: (b, i, 0)),
                  pl.BlockSpec((1, L, SB_W), lambda b, i: (b, 0, 0)),
                  pl.BlockSpec((1, L, SB_W), lambda b, i: (b, 0, 0)),
                  _const_spec((1, SB_W))],
        out_specs=pl.BlockSpec((1, SB_TQ, SB_W), lambda b, i: (b, i, 0)),
        out_shape=jax.ShapeDtypeStruct((B, L, SB_W), BF16),
        scratch_shapes=[pltpu.VMEM((SB_HEADS, SB_TQ, 2 * SB_DH), F32),
                        pltpu.VMEM((SB_HEADS, SB_TQ, 1), F32)],
        compiler_params=pltpu.CompilerParams(
            dimension_semantics=("parallel", "parallel"), vmem_limit_bytes=VMEM_LIMIT),
        name="sb_attn",
    )(stop, q, k, v, og)


def _gelu_tanh(x):
    return 0.5 * x * (1.0 + jnp.tanh(math.sqrt(2.0 / math.pi) * (x + 0.044715 * (x * x * x))))


def _cmul(ar, ai, br, bi):
    return ar * br - ai * bi, ar * bi + ai * br


def _s5_kernel(u_ref, bm_ref, cm_ref, pw_ref, d_ref, gw_ref, gb_ref, og_ref, out_ref, xs_s, y_s, carry_s, *, tl):
    l = pl.program_id(1)
    nseg = tl // 8
    nlt = S5_LANES // 128
    grp = 8

    @pl.when(l == 0)
    def _():
        carry_s[...] = jnp.zeros_like(carry_s)

    u = jnp.concatenate([u_ref[0, hf] for hf in range(S5_W // 128)], axis=-1)
    xs_s[...] = _dot(u.astype(BF16), bm_ref[...])

    def lanes(lt):
        return slice(lt * 128, (lt + 1) * 128), slice(S5_LANES + lt * 128, S5_LANES + (lt + 1) * 128)

    for g0 in range(0, nlt, grp):
        tiles = range(g0, g0 + grp)
        lam = [(pw_ref[0, :, lanes(lt)[0]], pw_ref[1, :, lanes(lt)[0]]) for lt in tiles]

        def local_step(j, st):
            r0 = pl.multiple_of(j * 8, 8)
            new = []
            for n, lt in enumerate(tiles):
                re_l, im_l = lanes(lt)
                pr, pi_ = _cmul(lam[n][0], lam[n][1], st[2 * n], st[2 * n + 1])
                xr = xs_s[pl.ds(r0, 8), re_l] + pr
                xi = xs_s[pl.ds(r0, 8), im_l] + pi_
                xs_s[pl.ds(r0, 8), re_l] = xr
                xs_s[pl.ds(r0, 8), im_l] = xi
                new += [xr, xi]
            return tuple(new)

        fin = lax.fori_loop(0, nseg, local_step, tuple(jnp.zeros((8, 128), F32) for _ in range(2 * grp)),
                            unroll=4)

        start = []
        for n, lt in enumerate(tiles):
            re_l, im_l = lanes(lt)
            er, ei = fin[2 * n], fin[2 * n + 1]
            for m, k in enumerate((1, 2, 4)):
                pr, pi_ = _cmul(pw_ref[2 + 2 * m, :, re_l], pw_ref[3 + 2 * m, :, re_l],
                                pltpu.roll(er, k, 0), pltpu.roll(ei, k, 0))
                er, ei = er + pr, ei + pi_
            s0r, s0i = carry_s[:, re_l], carry_s[:, im_l]
            pr, pi_ = _cmul(pw_ref[8, :, re_l], pw_ref[9, :, re_l], s0r, s0i)
            er, ei = er + pr, ei + pi_
            carry_s[:, re_l] = jnp.broadcast_to(er[7:8, :], (8, 128))
            carry_s[:, im_l] = jnp.broadcast_to(ei[7:8, :], (8, 128))
            row = lax.broadcasted_iota(jnp.int32, (8, 128), 0)
            start += [jnp.where(row == 0, s0r, pltpu.roll(er, 1, 0)),
                      jnp.where(row == 0, s0i, pltpu.roll(ei, 1, 0))]

        def fix_step(j, tr):
            r0 = pl.multiple_of(j * 8, 8)
            new = []
            for n, lt in enumerate(tiles):
                re_l, im_l = lanes(lt)
                pr, pi_ = _cmul(lam[n][0], lam[n][1], tr[2 * n], tr[2 * n + 1])
                xs_s[pl.ds(r0, 8), re_l] += pr
                xs_s[pl.ds(r0, 8), im_l] += pi_
                new += [pr, pi_]
            return tuple(new)

        lax.fori_loop(0, nseg, fix_step, tuple(start), unroll=4)

    rb = min(256, tl)
    for r0 in range(0, tl, rb):
        rows = slice(r0, r0 + rb)
        y = _dot(xs_s[rows, :].astype(BF16), cm_ref[...]) + d_ref[...] * u[rows, :]
        y = _gelu_tanh(y)
        y = y * _sigmoid(_dot(y.astype(BF16), gw_ref[...]) + gb_ref[...])
        ms = jnp.mean(y * y, axis=-1, keepdims=True)
        y = y * lax.rsqrt(ms + EPS) * og_ref[...]
        for hf in range(S5_W // 128):
            y_s[hf, rows, :] = y[:, hf * 128:(hf + 1) * 128]
    for r in range(8):
        out_ref[0, r * nseg:(r + 1) * nseg, :] = jnp.concatenate(
            [y_s[hf, pl.ds(r, nseg, stride=8), :] for hf in range(S5_W // 128)], axis=-1).astype(BF16)


def _s5(u, bm, cm, pw, d, gw, gb, og):
    B, _, L, _ = u.shape
    tl = min(S5_TL, L)
    grid = (B, L // tl)
    tspec = pl.BlockSpec((1, tl, S5_W), lambda b, i: (b, i, 0))
    uspec = pl.BlockSpec((1, S5_W // 128, tl, 128), lambda b, i: (b, 0, i, 0))
    return pl.pallas_call(
        functools.partial(_s5_kernel, tl=tl),
        grid=grid,
        in_specs=[uspec, _const_spec((S5_W, 2 * S5_LANES)), _const_spec((2 * S5_LANES, S5_W)),
                  _const_spec((10, 8, S5_LANES)), _const_spec((1, S5_W)), _const_spec((S5_W, S5_W)),
                  _const_spec((1, S5_W)), _const_spec((1, S5_W))],
        out_specs=tspec,
        out_shape=jax.ShapeDtypeStruct((B, L, S5_W), BF16),
        scratch_shapes=[pltpu.VMEM((tl, 2 * S5_LANES), F32), pltpu.VMEM((S5_W // 128, tl, 128), F32),
                        pltpu.VMEM((8, 2 * S5_LANES), F32)],
        compiler_params=pltpu.CompilerParams(
            dimension_semantics=("parallel", "arbitrary"), vmem_limit_bytes=VMEM_LIMIT),
        name="s5",
    )(u, bm, cm, pw, d, gw, gb, og)


def _ffn_kernel(x_ref, hm_ref, hs_ref, h5_ref, wom_ref, wos_ref, wo5_ref, g_ref,
                wu_ref, cw_ref, wd_ref, out_ref, h_s, gext_s, act_s, carry_s, *, tm):
    l = pl.program_id(1)

    @pl.when(l == 0)
    def _():
        carry_s[...] = jnp.zeros_like(carry_s)

    x1 = (x_ref[0] + _dot(hm_ref[0], wom_ref[...]) + _dot(hs_ref[0], wos_ref[...])
          + _dot(h5_ref[0], wo5_ref[...]))
    out_ref[0] = x1
    ms = jnp.mean(x1 * x1, axis=-1, keepdims=True)
    h_s[...] = (x1 * lax.rsqrt(ms + EPS) * g_ref[...]).astype(BF16)

    for c in range(D_FF // FFN_FC):
        cols = slice(c * FFN_FC, (c + 1) * FFN_FC)
        g = _dot(h_s[...], wu_ref[:, cols])
        v = _dot(h_s[...], wu_ref[:, D_FF + c * FFN_FC:D_FF + (c + 1) * FFN_FC])
        gext_s[0:8, :] = carry_s[c]
        gext_s[8:8 + tm, :] = g
        carry_s[c] = g[tm - 8:tm, :]
        conv = (cw_ref[2:3, cols] * g + cw_ref[1:2, cols] * gext_s[7:7 + tm, :]
                + cw_ref[0:1, cols] * gext_s[6:6 + tm, :])
        act_s[:, cols] = (conv * _sigmoid(conv) * v).astype(BF16)

    out_ref[0] += _dot(act_s[...], wd_ref[...])


def _ffn(x, hm, hs, h5, wom, wos, wo5, g, wu, cw, wd):
    B, L, _ = x.shape
    tm = min(FFN_TM, L)
    grid = (B, L // tm)
    tspec = lambda w_: pl.BlockSpec((1, tm, w_), lambda b, i: (b, i, 0))
    return pl.pallas_call(
        functools.partial(_ffn_kernel, tm=tm),
        grid=grid,
        in_specs=[tspec(D_MODEL), tspec(ML_WP), tspec(SB_W), tspec(S5_W),
                  _const_spec((ML_WP, D_MODEL)), _const_spec((SB_W, D_MODEL)), _const_spec((S5_W, D_MODEL)),
                  _const_spec((1, D_MODEL)), _const_spec((D_MODEL, 2 * D_FF)),
                  _const_spec((3, D_FF)), _const_spec((D_FF, D_MODEL))],
        out_specs=tspec(D_MODEL),
        out_shape=jax.ShapeDtypeStruct((B, L, D_MODEL), F32),
        scratch_shapes=[pltpu.VMEM((tm, D_MODEL), BF16), pltpu.VMEM((tm + 8, FFN_FC), F32),
                        pltpu.VMEM((tm, D_FF), BF16), pltpu.VMEM((D_FF // FFN_FC, 8, FFN_FC), F32)],
        compiler_params=pltpu.CompilerParams(
            dimension_semantics=("parallel", "arbitrary"), vmem_limit_bytes=VMEM_LIMIT),
        name="ffn",
    )(x, hm, hs, h5, wom, wos, wo5, g, wu, cw, wd)


def _pad_heads(w):
    lead = w.shape[:-1]
    w = w.reshape(lead + (ML_HEADS, ML_DH))
    w = jnp.pad(w, [(0, 0)] * len(lead) + [(0, 0), (0, ML_DHP - ML_DH)])
    return w.reshape(lead + (ML_WP,))


def _s5_params(a_re, a_im, log_dt, b_re, b_im, c_re, c_im, nseg):
    lr = jnp.minimum(a_re, -1e-4)
    li = a_im
    dt = jnp.exp(log_dt)[:, None]
    mag = jnp.exp(lr * dt)
    br = mag * jnp.cos(li * dt)
    bi = mag * jnp.sin(li * dt)
    nr, ni = br - 1.0, bi
    den = lr * lr + li * li
    fr = (nr * lr + ni * li) / den
    fi = (ni * lr - nr * li) / den
    bbr = fr[..., None] * b_re - fi[..., None] * b_im
    bbi = fr[..., None] * b_im + fi[..., None] * b_re
    eye = jnp.eye(S5_GROUPS, dtype=F32)
    bm_re = jnp.einsum("gk,gph->ghkp", eye, bbr).reshape(S5_W, S5_LANES)
    bm_im = jnp.einsum("gk,gph->ghkp", eye, bbi).reshape(S5_W, S5_LANES)
    bm = jnp.concatenate([bm_re, bm_im], axis=1).astype(BF16)
    cm_re = jnp.einsum("gk,ghp->gpkh", eye, c_re).reshape(S5_LANES, S5_W)
    cm_im = jnp.einsum("gk,ghp->gpkh", eye, c_im).reshape(S5_LANES, S5_W)
    cm = jnp.concatenate([cm_re, -cm_im], axis=0).astype(BF16)
    def cpow2(r, i, n):
        for _ in range(n):
            r, i = r * r - i * i, 2.0 * r * i
        return r, i

    lr_, li_ = br.reshape(-1), bi.reshape(-1)
    assert nseg & (nseg - 1) == 0
    ar, ai = cpow2(lr_, li_, nseg.bit_length() - 1)
    pr, pi_ = [ar], [ai]
    for _ in range(7):
        r, i = pr[-1], pi_[-1]
        pr.append(r * ar - i * ai)
        pi_.append(r * ai + i * ar)
    row = jnp.arange(8)[:, None]
    planes = [jnp.broadcast_to(lr_[None, :], (8, S5_LANES)), jnp.broadcast_to(li_[None, :], (8, S5_LANES))]
    for k in (1, 2, 4):
        keep = row >= k
        planes.append(jnp.where(keep, pr[k - 1][None, :], 0.0))
        planes.append(jnp.where(keep, pi_[k - 1][None, :], 0.0))
    planes.append(jnp.stack(pr, axis=0))
    planes.append(jnp.stack(pi_, axis=0))
    pw = jnp.stack(planes, axis=0).astype(F32)
    return bm, cm, pw


def kernel(x, norm_mix_g, w_in, ml_conv_w, ml_conv_b, ml_gate_b, ml_out_g, sb_q_g, sb_k_g, sb_out_g, s5_a_re, s5_a_im, s5_log_dt, s5_b_re, s5_b_im, s5_c_re, s5_c_im, s5_d, s5_glu_w, s5_glu_b, s5_out_g, w_out, norm_ffn_g, ffn_w_up, ffn_conv_w, ffn_w_down):
    depth = w_in.shape[0]
    w_up_b = ffn_w_up.astype(BF16)
    w_down_b = ffn_w_down.astype(BF16)
    w_out_b = w_out.astype(BF16)
    bm_all, cm_all, pw_all = jax.vmap(functools.partial(_s5_params, nseg=min(S5_TL, x.shape[1]) // 8))(
        s5_a_re, s5_a_im, s5_log_dt, s5_b_re, s5_b_im, s5_c_re, s5_c_im)
    seg = jnp.kron(jnp.eye(SB_HEADS, dtype=F32), jnp.ones((SB_DH, SB_DH), F32)).astype(BF16)
    for l in range(depth):
        w = w_in[l]
        o0 = 0
        parts = []
        for wd_ in (ML_W, ML_W, ML_W, ML_W, 2 * ML_HEADS, SB_W, SB_W, SB_W, S5_W):
            parts.append(w[:, o0:o0 + wd_])
            o0 += wd_
        wq, wk, wv_, wo_, wif, wsq, wsk, wsv, wu = parts
        wfi = jnp.concatenate([wif[:, ML_HEADS:], wif[:, :ML_HEADS]], axis=1)
        wv_pad = _pad_heads(wv_)
        wv_pad = wv_pad.at[:, GATE_LANE:GATE_LANE + 8].set(wif)
        wv_pad = wv_pad.at[:, ML_DHP + GATE_LANE:ML_DHP + GATE_LANE + 8].set(wfi)
        w_all = jnp.concatenate(
            [_pad_heads(wq), _pad_heads(wk), wv_pad, _pad_heads(wo_), wsq, wsk, wsv, wu], axis=1).astype(BF16)
        qg = jnp.tile(sb_q_g[l], SB_HEADS)[None, :] * (SB_DH ** -0.5 * math.log2(math.e))
        kg = jnp.tile(sb_k_g[l], SB_HEADS)[None, :]
        (qk, mv, mo, sq, sk, sv, u, gr1, gr2, gc1, gc2) = _in_proj(
            x, norm_mix_g[l][None, :], w_all, seg, qg, kg)

        cw = jnp.concatenate([_pad_heads(ml_conv_w[l][:, :ML_W]), _pad_heads(ml_conv_w[l][:, ML_W:])], axis=1)
        cb = jnp.concatenate([_pad_heads(ml_conv_b[l][:ML_W]), _pad_heads(ml_conv_b[l][ML_W:])])[None, :]
        gb = ml_gate_b[l]
        gb2 = jnp.concatenate([gb[ML_HEADS:], gb[:ML_HEADS]])
        hm = _mlstm(qk, mv, mo, gr1, gr2, gc1, gc2, cw, cb, gb[:, None], gb2[:, None], gb[None, :],
                    gb2[None, :], _pad_heads(ml_out_g[l].reshape(-1))[None, :])

        stop = (1.05 * SB_DH * jnp.max(jnp.abs(qg)) * jnp.max(jnp.abs(kg)) + 160.0).reshape(1).astype(F32)
        hs = _sb_attn(stop, sq, sk, sv, sb_out_g[l].reshape(1, SB_W))

        h5 = _s5(u, bm_all[l], cm_all[l], pw_all[l], s5_d[l].reshape(1, S5_W), s5_glu_w[l].astype(BF16),
                 s5_glu_b[l][None, :], s5_out_g[l][None, :])

        wo = w_out_b[l]
        wom = jnp.pad(wo[:ML_W].reshape(ML_HEADS, ML_DH, D_MODEL),
                      [(0, 0), (0, ML_DHP - ML_DH), (0, 0)]).reshape(ML_WP, D_MODEL)
        x = _ffn(x, hm, hs, h5, wom, wo[ML_W:ML_W + SB_W], wo[ML_W + SB_W:], norm_ffn_g[l][None, :],
                 w_up_b[l], ffn_conv_w[l], w_down_b[l])
    return x
```
